```python
import math
import jax, jax.numpy as jnp
from jax import lax
import numpy as np

D_MODEL = 1024
BATCH = 8
SEQ = 8192
DEPTH = 2

CHUNK = 64
D_FF = 2816
BRANCH_W = 512
N_BRANCH = 3
CONV_W = 3
H_RET = 4
DK_RET = BRANCH_W // H_RET
DV_RET = BRANCH_W // H_RET
H_ATT = 8
DH_ATT = BRANCH_W // H_ATT
N_PREV_CHUNKS = 8
BAND = (N_PREV_CHUNKS + 1) * CHUNK
REL_CLIP = 128
N_REL = 2 * REL_CLIP + 1
IN_COLS = 3 * BRANCH_W + 4 * BRANCH_W + 3 * BRANCH_W
EPS = 1e-6
NEG_INF = -1e30
ROPE_BASE = 10000.0

kernel_name = "hybrid_gated_conv_retention_chunkattn_macaron"


def _rmsnorm(x, w):
    xf = x.astype(jnp.float32)
    xf = xf * lax.rsqrt(jnp.mean(xf * xf, axis=-1, keepdims=True) + EPS)
    return (xf * w.astype(jnp.float32)).astype(x.dtype)


def _swiglu(h, w_gate, w_up, w_down):
    return (jax.nn.silu(h @ w_gate) * (h @ w_up)) @ w_down


def _short_gated_conv(u, b_gate, c_gate, conv_w):
    z = c_gate * u
    zp = jnp.pad(z, ((0, 0), (CONV_W - 1, 0), (0, 0)))
    s = z.shape[1]
    conv = sum(conv_w[j] * zp[:, j:j + s] for j in range(CONV_W))
    return b_gate * conv


def _rotary(x, cos, sin):
    half = x.shape[-1] // 2
    x1, x2 = x[..., :half], x[..., half:]
    c = cos[None, :, None, :]
    s_ = sin[None, :, None, :]
    return jnp.concatenate([x1 * c - x2 * s_, x1 * s_ + x2 * c], axis=-1).astype(x.dtype)


def _retention(q, k, v, g):
    bsz, s = q.shape[:2]
    nc = s // CHUNK
    log_gamma = jnp.log1p(-jnp.exp2(-5.0 - jnp.arange(H_RET, dtype=jnp.float32)))
    pos = jnp.arange(CHUNK, dtype=jnp.float32)
    d_intra = jnp.exp(log_gamma[:, None, None] * jnp.abs(pos[:, None] - pos[None, :]))
    q_decay = jnp.exp(log_gamma[:, None] * (pos + 1.0))
    k_decay = jnp.exp(log_gamma[:, None] * (CHUNK - 1.0 - pos))
    chunk_decay = jnp.exp(log_gamma * CHUNK)

    def to_chunks(t):
        return t.astype(jnp.float32).reshape(bsz, nc, CHUNK, H_RET, -1).transpose(1, 0, 3, 2, 4)

    qc = to_chunks(q) * (DK_RET ** -0.5)
    kc, vc = to_chunks(k), to_chunks(v)

    def step(state, qkv):
        qb, kb, vb = qkv
        inner = jnp.einsum('bhnk,bhmk->bhnm', qb, kb) * d_intra[None]
        o = jnp.einsum('bhnm,bhmv->bhnv', inner, vb) \
            + jnp.einsum('bhnk,bhkv->bhnv', qb * q_decay[None, :, :, None], state)
        state = state * chunk_decay[None, :, None, None] \
            + jnp.einsum('bhmk,bhmv->bhkv', kb * k_decay[None, :, :, None], vb)
        return state, o

    s0 = jnp.zeros((bsz, H_RET, DK_RET, DV_RET), jnp.float32)
    _, o = lax.scan(step, s0, (qc, kc, vc))
    o = o.transpose(1, 0, 3, 2, 4).reshape(bsz, s, H_RET, DV_RET)
    o = o * lax.rsqrt(jnp.mean(o * o, axis=-1, keepdims=True) + EPS)
    o = o.reshape(bsz, s, BRANCH_W).astype(g.dtype)
    return jax.nn.silu(g) * o


def _chunk_band_attention(q, k, v, rel_bias):
    bsz, s = q.shape[:2]
    nc = s // CHUNK
    pad = N_PREV_CHUNKS * CHUNK
    qc = q.reshape(bsz, nc, CHUNK, H_ATT, DH_ATT).transpose(1, 0, 3, 2, 4)
    kp = jnp.pad(k, ((0, 0), (pad, 0), (0, 0), (0, 0))).transpose(0, 2, 1, 3)
    vp = jnp.pad(v, ((0, 0), (pad, 0), (0, 0), (0, 0))).transpose(0, 2, 1, 3)
    n = jnp.arange(CHUNK)
    m = jnp.arange(BAND)
    dist = (pad + n)[:, None] - m[None, :]
    idx = jnp.clip(dist, -REL_CLIP, REL_CLIP) + REL_CLIP
    bias = rel_bias[:, idx].astype(jnp.float32)
    scale = DH_ATT ** -0.5

    def one_chunk(args):
        c, q_blk = args
        kb = lax.dynamic_slice_in_dim(kp, c * CHUNK, BAND, axis=2)
        vb = lax.dynamic_slice_in_dim(vp, c * CHUNK, BAND, axis=2)
        sc = jnp.einsum('bhnd,bhmd->bhnm', q_blk, kb).astype(jnp.float32) * scale + bias[None]
        valid = m >= (N_PREV_CHUNKS - c) * CHUNK
        sc = jnp.where(valid[None, None, None, :], sc, NEG_INF)
        p = jax.nn.softmax(sc, axis=-1).astype(vb.dtype)
        return jnp.einsum('bhnm,bhmd->bhnd', p, vb)

    out = lax.map(one_chunk, (jnp.arange(nc), qc))
    return out.transpose(1, 0, 3, 2, 4).reshape(bsz, s, BRANCH_W)


def setup_inputs(seed: int = 0) -> dict:
    key = jax.random.key(seed)
    ks = jax.random.split(key, 20)
    f32 = jnp.float32

    def w(k, shape, fan_in):
        return jax.random.normal(k, shape, f32) * (fan_in ** -0.5)

    def gain(k, shape):
        return 1.0 + 0.05 * jax.random.normal(k, shape, f32)

    return {
        "x": jax.random.normal(ks[0], (BATCH, SEQ, D_MODEL), f32),
        "ffn1_norm": gain(ks[1], (DEPTH, D_MODEL)),
        "ffn1_w_gate": w(ks[2], (DEPTH, D_MODEL, D_FF), D_MODEL),
        "ffn1_w_up": w(ks[3], (DEPTH, D_MODEL, D_FF), D_MODEL),
        "ffn1_w_down": w(ks[4], (DEPTH, D_FF, D_MODEL), D_FF),
        "mix_norm": gain(ks[5], (DEPTH, D_MODEL)),
        "w_in": w(ks[6], (DEPTH, D_MODEL, IN_COLS), D_MODEL),
        "conv_w": w(ks[7], (DEPTH, CONV_W, BRANCH_W), CONV_W),
        "rel_bias": 0.5 * jax.random.normal(ks[8], (DEPTH, H_ATT, N_REL), f32),
        "w_branch": w(ks[9], (DEPTH, N_BRANCH, BRANCH_W, D_MODEL), BRANCH_W),
        "w_merge_gate": w(ks[10], (DEPTH, N_BRANCH, D_MODEL, D_MODEL), D_MODEL),
        "w_out": w(ks[11], (DEPTH, D_MODEL, D_MODEL), D_MODEL),
        "ffn2_norm": gain(ks[12], (DEPTH, D_MODEL)),
        "ffn2_w_gate": w(ks[13], (DEPTH, D_MODEL, D_FF), D_MODEL),
        "ffn2_w_up": w(ks[14], (DEPTH, D_MODEL, D_FF), D_MODEL),
        "ffn2_w_down": w(ks[15], (DEPTH, D_FF, D_MODEL), D_FF),
        "final_norm": gain(ks[16], (D_MODEL,)),
    }


def reference(x, ffn1_norm, ffn1_w_gate, ffn1_w_up, ffn1_w_down, mix_norm, w_in, conv_w,
              rel_bias, w_branch, w_merge_gate, w_out, ffn2_norm, ffn2_w_gate, ffn2_w_up,
              ffn2_w_down, final_norm):
    bsz, s, _ = x.shape
    inv_freq = ROPE_BASE ** (-jnp.linspace(0.0, 1.0, DK_RET // 2, dtype=jnp.float32))
    ang = jnp.arange(s, dtype=jnp.float32)[:, None] * inv_freq[None, :]
    cos, sin = jnp.cos(ang), jnp.sin(ang)
    split_pts = [BRANCH_W * i for i in range(1, IN_COLS // BRANCH_W)]

    for l in range(DEPTH):
        x = x + 0.5 * _swiglu(_rmsnorm(x, ffn1_norm[l]), ffn1_w_gate[l], ffn1_w_up[l], ffn1_w_down[l])

        h = _rmsnorm(x, mix_norm[l])
        cols = jnp.split(h @ w_in[l], split_pts, axis=-1)
        cu, cb, cc, rq, rk, rv, rg, aq, ak, av = cols

        y_conv = _short_gated_conv(cu, cb, cc, conv_w[l])

        rq = _rotary(rq.reshape(bsz, s, H_RET, DK_RET), cos, sin)
        rk = _rotary(rk.reshape(bsz, s, H_RET, DK_RET), cos, sin)
        y_ret = _retention(rq, rk, rv.reshape(bsz, s, H_RET, DV_RET), rg)

        y_att = _chunk_band_attention(aq.reshape(bsz, s, H_ATT, DH_ATT),
                                      ak.reshape(bsz, s, H_ATT, DH_ATT),
                                      av.reshape(bsz, s, H_ATT, DH_ATT), rel_bias[l])

        merged = sum(jax.nn.sigmoid(h @ w_merge_gate[l, i]) * (y @ w_branch[l, i])
                     for i, y in enumerate((y_conv, y_ret, y_att)))
        x = x + merged @ w_out[l]

        x = x + 0.5 * _swiglu(_rmsnorm(x, ffn2_norm[l]), ffn2_w_gate[l], ffn2_w_up[l], ffn2_w_down[l])

    return _rmsnorm(x, final_norm)
```

```python
import functools

import jax
import jax.numpy as jnp
from jax import lax
from jax.experimental import pallas as pl
from jax.experimental.pallas import tpu as pltpu

F32 = jnp.float32
BF16 = jnp.bfloat16

CHUNK = 64
BRANCH_W = 512
H_RET = 4
DK_RET = BRANCH_W // H_RET
H_ATT = 8
DH_ATT = BRANCH_W // H_ATT
N_PREV_CHUNKS = 8
REL_CLIP = 128
EPS = 1e-6
NEG_INF = -1e30
ROPE_BASE = 10000.0

MXU_COLS = 256
VMEM_LIMIT_BYTES = 56 * 1024 * 1024

TOKEN_TILE = 512
FF_CHUNK = 256
RET_BLOCK = 256
ATT_QBLOCK = 512
ATT_SUB = 128
ATT_WIN = ATT_SUB + N_PREV_CHUNKS * CHUNK
HEAD_GROUP = MXU_COLS // DH_ATT


def _const_spec(shape):
    zeros = (0,) * len(shape)
    return pl.BlockSpec(shape, lambda *_: zeros, pipeline_mode=pl.Buffered(1))


def _params(n_axes):
    return pltpu.CompilerParams(dimension_semantics=("arbitrary",) * n_axes,
                                vmem_limit_bytes=VMEM_LIMIT_BYTES)


def _rms(x, w):
    ms = jnp.mean(x * x, axis=-1, keepdims=True)
    return x * lax.rsqrt(ms + EPS) * w


def _silu(g):
    return g * jax.nn.sigmoid(g)


def _ffn_body(*refs, final):
    if final:
        x_ref, nw_ref, wg_ref, wu_ref, wd_ref, fw_ref, o_ref, h_scr, a_scr = refs
    else:
        x_ref, nw_ref, wg_ref, wu_ref, wd_ref, o_ref, h_scr, a_scr = refs
    h_scr[...] = _rms(x_ref[...], nw_ref[...]).astype(BF16)
    d_ff = wg_ref.shape[1]
    for c in range(d_ff // FF_CHUNK):
        sl = slice(c * FF_CHUNK, (c + 1) * FF_CHUNK)
        g = jnp.dot(h_scr[...], wg_ref[:, sl], preferred_element_type=F32)
        u = jnp.dot(h_scr[...], wu_ref[:, sl], preferred_element_type=F32)
        a_scr[:, sl] = (_silu(g) * u).astype(BF16)
    y = x_ref[...] + 0.5 * jnp.dot(a_scr[...], wd_ref[...], preferred_element_type=F32)
    if final:
        y = _rms(y, fw_ref[...])
    o_ref[...] = y


def _ffn(x2d, nw, wg, wu, wd, fw=None):
    t, d = x2d.shape
    d_ff = wg.shape[1]
    tm = TOKEN_TILE
    final = fw is not None
    row = pl.BlockSpec((tm, d), lambda i: (i, 0))
    in_specs = [row, _const_spec((1, d)), _const_spec((d, d_ff)), _const_spec((d, d_ff)),
                _const_spec((d_ff, d))]
    args = [x2d, nw, wg, wu, wd]
    if final:
        in_specs.append(_const_spec((1, d)))
        args.append(fw)
    return pl.pallas_call(
        functools.partial(_ffn_body, final=final),
        grid=(t // tm,),
        in_specs=in_specs,
        out_specs=row,
        out_shape=jax.ShapeDtypeStruct((t, d), F32),
        scratch_shapes=[pltpu.VMEM((tm, d), BF16), pltpu.VMEM((tm, d_ff), BF16)],
        compiler_params=_params(1),
        name="ffn_final" if final else "ffn",
    )(*args)


def _inproj_body(x_ref, nw_ref, w_ref, cw_ref, cos_ref, sin_ref,
                 yc_ref, rq_ref, rk_ref, rv_ref, rg_ref, aq_ref, ak_ref, av_ref,
                 h_scr, z_scr, u_scr):
    tm = x_ref.shape[0]
    bw = BRANCH_W
    h_scr[...] = _rms(x_ref[...], nw_ref[...]).astype(BF16)

    def proj(g):
        return jnp.dot(h_scr[...], w_ref[:, g * bw:(g + 1) * bw], preferred_element_type=F32)

    @pl.when(pl.program_id(1) == 0)
    def _():
        z_scr[0:8] = jnp.zeros((8, bw), F32)

    @pl.when(pl.program_id(1) > 0)
    def _():
        z_scr[0:8] = z_scr[tm:tm + 8]

    u_scr[...] = proj(0)
    z_scr[8:tm + 8] = proj(2) * u_scr[...]
    conv = (cw_ref[0:1, :] * z_scr[6:tm + 6] + cw_ref[1:2, :] * z_scr[7:tm + 7]
            + cw_ref[2:3, :] * z_scr[8:tm + 8])
    yc_ref[...] = (proj(1) * conv).astype(BF16)

    cos2 = cos_ref[...]
    sin2 = sin_ref[...]
    for g, dst in ((3, rq_ref), (4, rk_ref)):
        r = proj(g)
        for hh in range(H_RET):
            sl = slice(hh * DK_RET, (hh + 1) * DK_RET)
            rh = r[:, sl]
            dst[:, sl] = (rh * cos2 + pltpu.roll(rh, DK_RET // 2, 1) * sin2).astype(BF16)
    rv_ref[...] = proj(5).astype(BF16)
    rg_ref[...] = proj(6).astype(BF16)
    aq_ref[...] = (proj(7) * (DH_ATT ** -0.5)).astype(BF16)
    ak_ref[...] = proj(8).astype(BF16)
    av_ref[...] = proj(9).astype(BF16)


def _inproj(x2d, bsz, nw, w_in, conv_w, cos2, sin2):
    t, d = x2d.shape
    s = t // bsz
    tm = TOKEN_TILE
    nt = s // tm
    bw = BRANCH_W
    row = lambda width: pl.BlockSpec((tm, width), lambda b, j: (b * nt + j, 0))
    pos = pl.BlockSpec((tm, DK_RET), lambda b, j: (j, 0))
    out_shape = [jax.ShapeDtypeStruct((t, bw), BF16)] * 8
    return pl.pallas_call(
        _inproj_body,
        grid=(bsz, nt),
        in_specs=[row(d), _const_spec((1, d)), _const_spec(w_in.shape), _const_spec(conv_w.shape),
                  pos, pos],
        out_specs=[row(bw)] * 8,
        out_shape=out_shape,
        scratch_shapes=[pltpu.VMEM((tm, d), BF16), pltpu.VMEM((tm + 8, bw), F32),
                        pltpu.VMEM((tm, bw), F32)],
        compiler_params=_params(2),
        name="inproj",
    )(x2d, nw, w_in, conv_w, cos2, sin2)


def _ret_body(q_ref, k_ref, v_ref, g_ref, wm_ref, qd_ref, kd_ref, cd_ref, y_ref, st_scr):
    @pl.when(pl.program_id(1) == 0)
    def _():
        st_scr[...] = jnp.zeros(st_scr.shape, F32)

    nt_dims = (((1,), (1,)), ((), ()))
    tn_dims = (((0,), (0,)), ((), ()))
    for hh in range(H_RET):
        sl = slice(hh * DK_RET, (hh + 1) * DK_RET)
        q = q_ref[:, sl]
        k = k_ref[:, sl]
        v = v_ref[:, sl]
        inner = lax.dot_general(q, k, nt_dims, preferred_element_type=F32) * wm_ref[hh]
        st = st_scr[hh]
        qdec = (q.astype(F32) * qd_ref[hh]).astype(BF16)
        o = (jnp.dot(inner.astype(BF16), v, preferred_element_type=F32)
             + jnp.dot(qdec, st.astype(BF16), preferred_element_type=F32))
        kdec = (k.astype(F32) * kd_ref[hh]).astype(BF16)
        st_scr[hh] = st * cd_ref[hh] + lax.dot_general(kdec, v, tn_dims, preferred_element_type=F32)
        o = o * lax.rsqrt(jnp.mean(o * o, axis=-1, keepdims=True) + EPS)
        y_ref[:, sl] = (_silu(g_ref[:, sl].astype(F32)) * o).astype(BF16)


def _retention_tables(blk):
    log_gamma = jnp.log1p(-jnp.exp2(-5.0 - jnp.arange(H_RET, dtype=F32)))
    pos = jnp.arange(blk, dtype=F32)
    chunk = jnp.arange(blk) // CHUNK
    scale = DK_RET ** -0.5
    decay = jnp.exp(log_gamma[:, None, None] * jnp.abs(pos[:, None] - pos[None, :]))
    wm = jnp.where(chunk[None, :, None] >= chunk[None, None, :], decay, 0.0) * scale
    qd = jnp.exp(log_gamma[:, None] * (pos + 1.0)) * scale
    kd = jnp.exp(log_gamma[:, None] * (blk - 1.0 - pos))
    cd = jnp.exp(log_gamma * blk)
    bc = lambda a: jnp.broadcast_to(a[..., None], a.shape + (DK_RET,))
    return wm, bc(qd), bc(kd), bc(cd)[:, None, :]


def _retention(rq, rk, rv, rg, bsz):
    t, bw = rq.shape
    s = t // bsz
    blk = RET_BLOCK
    nb = s // blk
    wm, qd, kd, cd = _retention_tables(blk)
    row = pl.BlockSpec((blk, bw), lambda b, i: (b * nb + i, 0))
    return pl.pallas_call(
        _ret_body,
        grid=(bsz, nb),
        in_specs=[row, row, row, row, _const_spec(wm.shape), _const_spec(qd.shape),
                  _const_spec(kd.shape), _const_spec(cd.shape)],
        out_specs=row,
        out_shape=jax.ShapeDtypeStruct((t, bw), BF16),
        scratch_shapes=[pltpu.VMEM((H_RET, DK_RET, DK_RET), F32)],
        compiler_params=_params(2),
        name="retention",
    )(rq, rk, rv, rg, wm, qd, kd, cd)


def _att_body(q_ref, kp_ref, kc_ref, vp_ref, vc_ref, bias_ref, o_ref, k_scr, v_scr):
    qb = q_ref.shape[0]
    i = pl.program_id(1)
    k_scr[0:qb] = kp_ref[...]
    k_scr[qb:2 * qb] = kc_ref[...]
    v_scr[0:qb] = vp_ref[...]
    v_scr[qb:2 * qb] = vc_ref[...]
    lane_head = lax.broadcasted_iota(jnp.int32, (ATT_SUB, MXU_COLS), 1) // DH_ATT
    col = lax.broadcasted_iota(jnp.int32, (ATT_SUB, ATT_WIN), 1)
    nt_dims = (((1,), (1,)), ((), ()))

    def sub(t, carry):
        r0 = pl.multiple_of(t * ATT_SUB, ATT_SUB)
        valid = col >= (1 - i) * qb - t * ATT_SUB
        for g in range(H_ATT // HEAD_GROUP):
            cs = slice(g * MXU_COLS, (g + 1) * MXU_COLS)
            qg = q_ref[pl.ds(r0, ATT_SUB), cs].astype(F32)
            qs = jnp.concatenate(
                [jnp.where(lane_head == hh, qg, 0.0).astype(BF16) for hh in range(HEAD_GROUP)], axis=0)
            sc = lax.dot_general(qs, k_scr[pl.ds(r0, ATT_WIN), cs], nt_dims,
                                 preferred_element_type=F32)
            es, inv = [], []
            for hh in range(HEAD_GROUP):
                sh = sc[hh * ATT_SUB:(hh + 1) * ATT_SUB] + bias_ref[g * HEAD_GROUP + hh]
                sh = jnp.where(valid, sh, NEG_INF)
                e = jnp.exp(sh - jnp.max(sh, axis=-1, keepdims=True))
                inv.append(1.0 / jnp.sum(e, axis=-1, keepdims=True))
                es.append(e.astype(BF16))
            pv = jnp.dot(jnp.concatenate(es, axis=0), v_scr[pl.ds(r0, ATT_WIN), cs],
                         preferred_element_type=F32)
            og = jnp.zeros((ATT_SUB, MXU_COLS), F32)
            for hh in range(HEAD_GROUP):
                og = jnp.where(lane_head == hh, pv[hh * ATT_SUB:(hh + 1) * ATT_SUB] * inv[hh], og)
            o_ref[pl.ds(r0, ATT_SUB), cs] = og.astype(BF16)
        return carry

    lax.fori_loop(0, qb // ATT_SUB, sub, 0)


def _attention_bias(rel_bias):
    n = jnp.arange(ATT_SUB)
    m = jnp.arange(ATT_WIN)
    dist = (N_PREV_CHUNKS * CHUNK + n)[:, None] - m[None, :]
    idx = jnp.clip(dist, -REL_CLIP, REL_CLIP) + REL_CLIP
    qc = (n // CHUNK)[:, None]
    kc = (m // CHUNK)[None, :]
    band = (kc >= qc) & (kc <= qc + N_PREV_CHUNKS)
    return jnp.where(band[None], rel_bias[:, idx].astype(F32), NEG_INF)


def _attention(aq, ak, av, rel_bias, bsz):
    t, bw = aq.shape
    s = t // bsz
    qb = ATT_QBLOCK
    nb = s // qb
    bias = _attention_bias(rel_bias)
    cur = pl.BlockSpec((qb, bw), lambda b, i: (b * nb + i, 0))
    prev = pl.BlockSpec((qb, bw), lambda b, i: (b * nb + jnp.maximum(i - 1, 0), 0))
    return pl.pallas_call(
        _att_body,
        grid=(bsz, nb),
        in_specs=[cur, prev, cur, prev, cur, _const_spec(bias.shape)],
        out_specs=cur,
        out_shape=jax.ShapeDtypeStruct((t, bw), BF16),
        scratch_shapes=[pltpu.VMEM((2 * qb, bw), BF16), pltpu.VMEM((2 * qb, bw), BF16)],
        compiler_params=_params(2),
        name="attention",
    )(aq, ak, ak, av, av, bias)


def _merge_body(x_ref, nw_ref, yc_ref, yr_ref, ya_ref, wg_ref, wb_ref, wo_ref, o_ref, h_scr, m_scr):
    d = x_ref.shape[1]
    h_scr[...] = _rms(x_ref[...], nw_ref[...]).astype(BF16)
    branches = (yc_ref, yr_ref, ya_ref)
    for c in range(d // MXU_COLS):
        sl = slice(c * MXU_COLS, (c + 1) * MXU_COLS)
        acc = None
        for b, y_ref in enumerate(branches):
            gate = jax.nn.sigmoid(jnp.dot(h_scr[...], wg_ref[b, :, sl], preferred_element_type=F32))
            term = gate * jnp.dot(y_ref[...], wb_ref[b, :, sl], preferred_element_type=F32)
            acc = term if acc is None else acc + term
        m_scr[:, sl] = acc.astype(BF16)
    o_ref[...] = x_ref[...] + jnp.dot(m_scr[...], wo_ref[...], preferred_element_type=F32)


def _merge(x2d, nw, yc, yr, ya, w_gate, w_branch, w_out):
    t, d = x2d.shape
    tm = TOKEN_TILE
    bw = BRANCH_W
    row = lambda width: pl.BlockSpec((tm, width), lambda i: (i, 0))
    return pl.pallas_call(
        _merge_body,
        grid=(t // tm,),
        in_specs=[row(d), _const_spec((1, d)), row(bw), row(bw), row(bw),
                  _const_spec(w_gate.shape), _const_spec(w_branch.shape), _const_spec(w_out.shape)],
        out_specs=row(d),
        out_shape=jax.ShapeDtypeStruct((t, d), F32),
        scratch_shapes=[pltpu.VMEM((tm, d), BF16), pltpu.VMEM((tm, d), BF16)],
        compiler_params=_params(1),
        name="merge",
    )(x2d, nw, yc, yr, ya, w_gate, w_branch, w_out)


def kernel(x, ffn1_norm, ffn1_w_gate, ffn1_w_up, ffn1_w_down, mix_norm, w_in, conv_w, rel_bias,
           w_branch, w_merge_gate, w_out, ffn2_norm, ffn2_w_gate, ffn2_w_up, ffn2_w_down, final_norm):
    bsz, s, d = x.shape
    depth = w_in.shape[0]
    assert s % TOKEN_TILE == 0 and s % ATT_QBLOCK == 0 and s % RET_BLOCK == 0
    assert ATT_QBLOCK == N_PREV_CHUNKS * CHUNK and RET_BLOCK % CHUNK == 0

    inv_freq = ROPE_BASE ** (-jnp.linspace(0.0, 1.0, DK_RET // 2, dtype=F32))
    ang = jnp.arange(s, dtype=F32)[:, None] * inv_freq[None, :]
    cos, sin = jnp.cos(ang), jnp.sin(ang)
    cos2 = jnp.concatenate([cos, cos], axis=-1)
    sin2 = jnp.concatenate([-sin, sin], axis=-1)

    vec = lambda w: w.reshape(1, d).astype(F32)
    bf = lambda w: w.astype(BF16)
    xf = x.reshape(bsz * s, d)
    for l in range(depth):
        xf = _ffn(xf, vec(ffn1_norm[l]), bf(ffn1_w_gate[l]), bf(ffn1_w_up[l]), bf(ffn1_w_down[l]))
        yc, rq, rk, rv, rg, aq, ak, av = _inproj(xf, bsz, vec(mix_norm[l]), bf(w_in[l]),
                                                 conv_w[l].astype(F32), cos2, sin2)
        yr = _retention(rq, rk, rv, rg, bsz)
        ya = _attention(aq, ak, av, rel_bias[l], bsz)
        xf = _merge(xf, vec(mix_norm[l]), yc, yr, ya, bf(w_merge_gate[l]), bf(w_branch[l]), bf(w_out[l]))
        xf = _ffn(xf, vec(ffn2_norm[l]), bf(ffn2_w_gate[l]), bf(ffn2_w_up[l]), bf(ffn2_w_down[l]),
                  fw=vec(final_norm) if l == depth - 1 else None)
    return xf.reshape(bsz, s, d)
```

```python
import functools

import jax
import jax.numpy as jnp
from jax import lax
from jax.experimental import pallas as pl
from jax.experimental.pallas import tpu as pltpu

F32 = jnp.float32
BF16 = jnp.bfloat16

CHUNK = 64
BRANCH_W = 512
H_RET = 4
DK_RET = BRANCH_W // H_RET
H_ATT = 8
DH_ATT = BRANCH_W // H_ATT
N_PREV_CHUNKS = 8
REL_CLIP = 128
EPS = 1e-6
NEG_INF = -1e30
ROPE_BASE = 10000.0

MXU_COLS = 256
VMEM_LIMIT_BYTES = 56 * 1024 * 1024

TOKEN_TILE = 1024
FF_CHUNK = 256
RET_BLOCK = 256
RET_STEP = 512
ATT_QBLOCK = 512
ATT_SUB = 128
ATT_WIN = ATT_SUB + N_PREV_CHUNKS * CHUNK
HEAD_GROUP = MXU_COLS // DH_ATT


def _const_spec(shape):
    zeros = (0,) * len(shape)
    return pl.BlockSpec(shape, lambda *_: zeros, pipeline_mode=pl.Buffered(1))


def _params(n_axes):
    return pltpu.CompilerParams(dimension_semantics=("arbitrary",) * n_axes,
                                vmem_limit_bytes=VMEM_LIMIT_BYTES)


def _rms(x, w):
    ms = jnp.mean(x * x, axis=-1, keepdims=True)
    return x * lax.rsqrt(ms + EPS) * w


def _silu(g):
    return g * jax.nn.sigmoid(g)


def _ffn_body(*refs, final):
    if final:
        x_ref, nw_ref, wg_ref, wu_ref, wd_ref, fw_ref, o_ref, h_scr, a_scr = refs
    else:
        x_ref, nw_ref, wg_ref, wu_ref, wd_ref, o_ref, h_scr, a_scr = refs
    h_scr[...] = _rms(x_ref[...], nw_ref[...]).astype(BF16)
    d_ff = wg_ref.shape[1]
    for c in range(d_ff // FF_CHUNK):
        sl = slice(c * FF_CHUNK, (c + 1) * FF_CHUNK)
        g = jnp.dot(h_scr[...], wg_ref[:, sl], preferred_element_type=F32)
        u = jnp.dot(h_scr[...], wu_ref[:, sl], preferred_element_type=F32)
        a_scr[:, sl] = (_silu(g) * u).astype(BF16)
    y = x_ref[...] + 0.5 * jnp.dot(a_scr[...], wd_ref[...], preferred_element_type=F32)
    if final:
        y = _rms(y, fw_ref[...])
    o_ref[...] = y


def _ffn(x2d, nw, wg, wu, wd, fw=None):
    t, d = x2d.shape
    d_ff = wg.shape[1]
    tm = TOKEN_TILE
    final = fw is not None
    row = pl.BlockSpec((tm, d), lambda i: (i, 0))
    in_specs = [row, _const_spec((1, d)), _const_spec((d, d_ff)), _const_spec((d, d_ff)),
                _const_spec((d_ff, d))]
    args = [x2d, nw, wg, wu, wd]
    if final:
        in_specs.append(_const_spec((1, d)))
        args.append(fw)
    return pl.pallas_call(
        functools.partial(_ffn_body, final=final),
        grid=(t // tm,),
        in_specs=in_specs,
        out_specs=row,
        out_shape=jax.ShapeDtypeStruct((t, d), F32),
        scratch_shapes=[pltpu.VMEM((tm, d), BF16), pltpu.VMEM((tm, d_ff), BF16)],
        compiler_params=_params(1),
        name="ffn_final" if final else "ffn",
    )(*args)


def _inproj_body(x_ref, nw_ref, w_ref, cw_ref, cos_ref, sin_ref,
                 yc_ref, rq_ref, rk_ref, rv_ref, rg_ref, aq_ref, ak_ref, av_ref,
                 h_scr, z_scr, u_scr):
    tm = x_ref.shape[0]
    bw = BRANCH_W

    @pl.when(pl.program_id(1) == 0)
    def _():
        z_scr[0:8] = jnp.zeros((8, bw), F32)

    @pl.when(pl.program_id(1) > 0)
    def _():
        z_scr[0:8] = z_scr[tm:tm + 8]

    h_scr[...] = _rms(x_ref[...], nw_ref[...]).astype(BF16)

    def proj(g):
        return jnp.dot(h_scr[...], w_ref[:, g * bw:(g + 1) * bw], preferred_element_type=F32)

    u_scr[...] = proj(0)
    z_scr[8:tm + 8] = proj(2) * u_scr[...]
    conv = (cw_ref[0:1, :] * z_scr[6:tm + 6] + cw_ref[1:2, :] * z_scr[7:tm + 7]
            + cw_ref[2:3, :] * z_scr[8:tm + 8])
    yc_ref[...] = (proj(1) * conv).astype(BF16)

    cos2 = cos_ref[...]
    sin2 = sin_ref[...]
    for g, dst in ((3, rq_ref), (4, rk_ref)):
        r = proj(g)
        for hh in range(H_RET):
            sl = slice(hh * DK_RET, (hh + 1) * DK_RET)
            rh = r[:, sl]
            dst[:, sl] = (rh * cos2 + pltpu.roll(rh, DK_RET // 2, 1) * sin2).astype(BF16)
    rv_ref[...] = proj(5).astype(BF16)
    rg_ref[...] = proj(6).astype(BF16)
    aq_ref[...] = (proj(7) * (DH_ATT ** -0.5)).astype(BF16)
    ak_ref[...] = proj(8).astype(BF16)
    av_ref[...] = proj(9).astype(BF16)


def _inproj(x2d, bsz, nw, w_in, conv_w, cos2, sin2):
    t, d = x2d.shape
    s = t // bsz
    tm = TOKEN_TILE
    nt = s // tm
    bw = BRANCH_W
    row = lambda width: pl.BlockSpec((tm, width), lambda b, j: (b * nt + j, 0))
    pos = pl.BlockSpec((tm, DK_RET), lambda b, j: (j, 0))
    out_shape = [jax.ShapeDtypeStruct((t, bw), BF16)] * 8
    return pl.pallas_call(
        _inproj_body,
        grid=(bsz, nt),
        in_specs=[row(d), _const_spec((1, d)), _const_spec(w_in.shape), _const_spec(conv_w.shape),
                  pos, pos],
        out_specs=[row(bw)] * 8,
        out_shape=out_shape,
        scratch_shapes=[pltpu.VMEM((tm, d), BF16), pltpu.VMEM((tm + 8, bw), F32),
                        pltpu.VMEM((tm, bw), F32)],
        compiler_params=_params(2),
        name="inproj",
    )(x2d, nw, w_in, conv_w, cos2, sin2)


def _ret_body(q_ref, k_ref, v_ref, g_ref, wm_ref, qd_ref, kd_ref, cd_ref, y_ref, st_scr):
    @pl.when(pl.program_id(1) == 0)
    def _():
        st_scr[...] = jnp.zeros(st_scr.shape, F32)

    nt_dims = (((1,), (1,)), ((), ()))
    tn_dims = (((0,), (0,)), ((), ()))
    blk = wm_ref.shape[1]
    for b in range(q_ref.shape[0] // blk):
        rows = slice(b * blk, (b + 1) * blk)
        for hh in range(H_RET):
            sl = slice(hh * DK_RET, (hh + 1) * DK_RET)
            q = q_ref[rows, sl]
            k = k_ref[rows, sl]
            v = v_ref[rows, sl]
            inner = lax.dot_general(q, k, nt_dims, preferred_element_type=F32) * wm_ref[hh]
            st = st_scr[hh]
            o = (jnp.dot(inner.astype(BF16), v, preferred_element_type=F32)
                 + jnp.dot(q, st.astype(BF16), preferred_element_type=F32) * qd_ref[hh])
            kdec = (k.astype(F32) * kd_ref[hh]).astype(BF16)
            st_scr[hh] = st * cd_ref[hh] + lax.dot_general(kdec, v, tn_dims,
                                                           preferred_element_type=F32)
            o = o * lax.rsqrt(jnp.mean(o * o, axis=-1, keepdims=True) + EPS)
            y_ref[rows, sl] = (_silu(g_ref[rows, sl].astype(F32)) * o).astype(BF16)


def _retention_tables(blk):
    log_gamma = jnp.log1p(-jnp.exp2(-5.0 - jnp.arange(H_RET, dtype=F32)))
    pos = jnp.arange(blk, dtype=F32)
    chunk = jnp.arange(blk) // CHUNK
    scale = DK_RET ** -0.5
    decay = jnp.exp(log_gamma[:, None, None] * jnp.abs(pos[:, None] - pos[None, :]))
    wm = jnp.where(chunk[None, :, None] >= chunk[None, None, :], decay, 0.0) * scale
    qd = jnp.exp(log_gamma[:, None] * (pos + 1.0)) * scale
    kd = jnp.exp(log_gamma[:, None] * (blk - 1.0 - pos))
    cd = jnp.exp(log_gamma * blk)
    bc = lambda a: jnp.broadcast_to(a[..., None], a.shape + (DK_RET,))
    return wm, bc(qd), bc(kd), bc(cd)[:, None, :]


def _retention(rq, rk, rv, rg, bsz):
    t, bw = rq.shape
    s = t // bsz
    step = RET_STEP
    nb = s // step
    wm, qd, kd, cd = _retention_tables(RET_BLOCK)
    row = pl.BlockSpec((step, bw), lambda b, i: (b * nb + i, 0))
    return pl.pallas_call(
        _ret_body,
        grid=(bsz, nb),
        in_specs=[row, row, row, row, _const_spec(wm.shape), _const_spec(qd.shape),
                  _const_spec(kd.shape), _const_spec(cd.shape)],
        out_specs=row,
        out_shape=jax.ShapeDtypeStruct((t, bw), BF16),
        scratch_shapes=[pltpu.VMEM((H_RET, DK_RET, DK_RET), F32)],
        compiler_params=_params(2),
        name="retention",
    )(rq, rk, rv, rg, wm, qd, kd, cd)


def _att_body(q_ref, kp_ref, kc_ref, vp_ref, vc_ref, bias_ref, o_ref, k_scr, v_scr):
    qb = q_ref.shape[0]
    i = pl.program_id(1)
    k_scr[0:qb] = kp_ref[...]
    k_scr[qb:2 * qb] = kc_ref[...]
    v_scr[0:qb] = vp_ref[...]
    v_scr[qb:2 * qb] = vc_ref[...]
    lane_head = lax.broadcasted_iota(jnp.int32, (ATT_SUB, MXU_COLS), 1) // DH_ATT
    col = lax.broadcasted_iota(jnp.int32, (ATT_SUB, ATT_WIN), 1)
    nt_dims = (((1,), (1,)), ((), ()))

    for t in range(qb // ATT_SUB):
        r0 = t * ATT_SUB
        valid = col >= (1 - i) * qb - t * ATT_SUB
        for g in range(H_ATT // HEAD_GROUP):
            cs = slice(g * MXU_COLS, (g + 1) * MXU_COLS)
            qg = q_ref[pl.ds(r0, ATT_SUB), cs].astype(F32)
            qs = jnp.concatenate(
                [jnp.where(lane_head == hh, qg, 0.0).astype(BF16) for hh in range(HEAD_GROUP)], axis=0)
            sc = lax.dot_general(qs, k_scr[pl.ds(r0, ATT_WIN), cs], nt_dims,
                                 preferred_element_type=F32)
            es, inv = [], []
            for hh in range(HEAD_GROUP):
                sh = sc[hh * ATT_SUB:(hh + 1) * ATT_SUB] + bias_ref[g * HEAD_GROUP + hh]
                sh = jnp.where(valid, sh, NEG_INF)
                e = jnp.exp(sh - jnp.max(sh, axis=-1, keepdims=True))
                inv.append(1.0 / jnp.sum(e, axis=-1, keepdims=True))
                es.append(e.astype(BF16))
            pv = jnp.dot(jnp.concatenate(es, axis=0), v_scr[pl.ds(r0, ATT_WIN), cs],
                         preferred_element_type=F32)
            og = jnp.zeros((ATT_SUB, MXU_COLS), F32)
            for hh in range(HEAD_GROUP):
                og = jnp.where(lane_head == hh, pv[hh * ATT_SUB:(hh + 1) * ATT_SUB] * inv[hh], og)
            o_ref[pl.ds(r0, ATT_SUB), cs] = og.astype(BF16)


def _attention_bias(rel_bias):
    h = rel_bias.shape[0]
    back = N_PREV_CHUNKS * CHUNK
    period = ATT_WIN + ATT_SUB
    far = rel_bias[:, -1:].astype(F32)
    row = jnp.concatenate([jnp.broadcast_to(far, (h, back - REL_CLIP)),
                           rel_bias[:, ::-1].astype(F32),
                           jnp.broadcast_to(far, (h, period - back - REL_CLIP - 1))], axis=1)
    flat = jnp.broadcast_to(row[:, None, :], (h, ATT_SUB, period)).reshape(h, ATT_SUB * period)
    skew = flat[:, :ATT_SUB * (period - 1)].reshape(h, ATT_SUB, period - 1)[:, :, :ATT_WIN]
    qc = (jnp.arange(ATT_SUB) // CHUNK)[:, None]
    kc = (jnp.arange(ATT_WIN) // CHUNK)[None, :]
    band = (kc >= qc) & (kc <= qc + N_PREV_CHUNKS)
    return jnp.where(band[None], skew, NEG_INF)


def _attention(aq, ak, av, rel_bias, bsz):
    t, bw = aq.shape
    s = t // bsz
    qb = ATT_QBLOCK
    nb = s // qb
    bias = _attention_bias(rel_bias)
    cur = pl.BlockSpec((qb, bw), lambda b, i: (b * nb + i, 0))
    prev = pl.BlockSpec((qb, bw), lambda b, i: (b * nb + jnp.maximum(i - 1, 0), 0))
    return pl.pallas_call(
        _att_body,
        grid=(bsz, nb),
        in_specs=[cur, prev, cur, prev, cur, _const_spec(bias.shape)],
        out_specs=cur,
        out_shape=jax.ShapeDtypeStruct((t, bw), BF16),
        scratch_shapes=[pltpu.VMEM((2 * qb, bw), BF16), pltpu.VMEM((2 * qb, bw), BF16)],
        compiler_params=_params(2),
        name="attention",
    )(aq, ak, ak, av, av, bias)


def _merge_body(x_ref, nw_ref, yc_ref, yr_ref, ya_ref, wg_ref, wb_ref, wo_ref, o_ref, h_scr, m_scr):
    d = x_ref.shape[1]
    h_scr[...] = _rms(x_ref[...], nw_ref[...]).astype(BF16)
    branches = (yc_ref, yr_ref, ya_ref)
    for c in range(d // MXU_COLS):
        sl = slice(c * MXU_COLS, (c + 1) * MXU_COLS)
        acc = None
        for b, y_ref in enumerate(branches):
            gate = jax.nn.sigmoid(jnp.dot(h_scr[...], wg_ref[b, :, sl], preferred_element_type=F32))
            term = gate * jnp.dot(y_ref[...], wb_ref[b, :, sl], preferred_element_type=F32)
            acc = term if acc is None else acc + term
        m_scr[:, sl] = acc.astype(BF16)
    o_ref[...] = x_ref[...] + jnp.dot(m_scr[...], wo_ref[...], preferred_element_type=F32)


def _merge(x2d, nw, yc, yr, ya, w_gate, w_branch, w_out):
    t, d = x2d.shape
    tm = TOKEN_TILE
    bw = BRANCH_W
    row = lambda width: pl.BlockSpec((tm, width), lambda i: (i, 0))
    return pl.pallas_call(
        _merge_body,
        grid=(t // tm,),
        in_specs=[row(d), _const_spec((1, d)), row(bw), row(bw), row(bw),
                  _const_spec(w_gate.shape), _const_spec(w_branch.shape), _const_spec(w_out.shape)],
        out_specs=row(d),
        out_shape=jax.ShapeDtypeStruct((t, d), F32),
        scratch_shapes=[pltpu.VMEM((tm, d), BF16), pltpu.VMEM((tm, d), BF16)],
        compiler_params=_params(1),
        name="merge",
    )(x2d, nw, yc, yr, ya, w_gate, w_branch, w_out)


def kernel(x, ffn1_norm, ffn1_w_gate, ffn1_w_up, ffn1_w_down, mix_norm, w_in, conv_w, rel_bias,
           w_branch, w_merge_gate, w_out, ffn2_norm, ffn2_w_gate, ffn2_w_up, ffn2_w_down, final_norm):
    bsz, s, d = x.shape
    depth = w_in.shape[0]
    assert s % TOKEN_TILE == 0 and s % ATT_QBLOCK == 0 and s % RET_STEP == 0
    assert ATT_QBLOCK == N_PREV_CHUNKS * CHUNK and RET_BLOCK % CHUNK == 0
    assert RET_STEP % RET_BLOCK == 0

    inv_freq = ROPE_BASE ** (-jnp.linspace(0.0, 1.0, DK_RET // 2, dtype=F32))
    ang = jnp.arange(s, dtype=F32)[:, None] * inv_freq[None, :]
    cos, sin = jnp.cos(ang), jnp.sin(ang)
    cos2 = jnp.concatenate([cos, cos], axis=-1)
    sin2 = jnp.concatenate([-sin, sin], axis=-1)

    vec = lambda w: w.reshape(1, d).astype(F32)
    bf = lambda w: w.astype(BF16)
    xf = x.reshape(bsz * s, d)
    for l in range(depth):
        xf = _ffn(xf, vec(ffn1_norm[l]), bf(ffn1_w_gate[l]), bf(ffn1_w_up[l]), bf(ffn1_w_down[l]))
        yc, rq, rk, rv, rg, aq, ak, av = _inproj(xf, bsz, vec(mix_norm[l]), bf(w_in[l]),
                                                 conv_w[l].astype(F32), cos2, sin2)
        yr = _retention(rq, rk, rv, rg, bsz)
        ya = _attention(aq, ak, av, rel_bias[l], bsz)
        xf = _merge(xf, vec(mix_norm[l]), yc, yr, ya, bf(w_merge_gate[l]), bf(w_branch[l]), bf(w_out[l]))
        xf = _ffn(xf, vec(ffn2_norm[l]), bf(ffn2_w_gate[l]), bf(ffn2_w_up[l]), bf(ffn2_w_down[l]),
                  fw=vec(final_norm) if l == depth - 1 else None)
    return xf.reshape(bsz, s, d)
```

```python
import functools

import jax
import jax.numpy as jnp
from jax import lax
from jax.experimental import pallas as pl
from jax.experimental.pallas import tpu as pltpu

F32 = jnp.float32
BF16 = jnp.bfloat16

CHUNK = 64
BRANCH_W = 512
H_RET = 4
DK_RET = BRANCH_W // H_RET
H_ATT = 8
DH_ATT = BRANCH_W // H_ATT
N_PREV_CHUNKS = 8
REL_CLIP = 128
EPS = 1e-6
NEG_INF = -1e30
ROPE_BASE = 10000.0

MXU_COLS = 256
VMEM_LIMIT_BYTES = 56 * 1024 * 1024

FFN_TILE = 1024
FF_CHUNK = 256
MIX_TILE = N_PREV_CHUNKS * CHUNK
RET_BLOCK = 256
ATT_SUB = 128
ATT_WIN = ATT_SUB + N_PREV_CHUNKS * CHUNK
HEAD_GROUP = MXU_COLS // DH_ATT

NT_DIMS = (((1,), (1,)), ((), ()))
TN_DIMS = (((0,), (0,)), ((), ()))


def _const_spec(shape):
    zeros = (0,) * len(shape)
    return pl.BlockSpec(shape, lambda *_: zeros, pipeline_mode=pl.Buffered(1))


def _params(n_axes):
    return pltpu.CompilerParams(dimension_semantics=("arbitrary",) * n_axes,
                                vmem_limit_bytes=VMEM_LIMIT_BYTES)


def _rms(x, w):
    ms = jnp.mean(x * x, axis=-1, keepdims=True)
    return x * lax.rsqrt(ms + EPS) * w


def _silu(g):
    return g * jax.nn.sigmoid(g)


def _ffn_body(*refs, final):
    if final:
        x_ref, nw_ref, wg_ref, wu_ref, wd_ref, fw_ref, o_ref, h_scr, a_scr = refs
    else:
        x_ref, nw_ref, wg_ref, wu_ref, wd_ref, o_ref, h_scr, a_scr = refs
    h_scr[...] = _rms(x_ref[...], nw_ref[...]).astype(BF16)
    d_ff = wg_ref.shape[1]
    for c in range(d_ff // FF_CHUNK):
        sl = slice(c * FF_CHUNK, (c + 1) * FF_CHUNK)
        g = jnp.dot(h_scr[...], wg_ref[:, sl], preferred_element_type=F32)
        u = jnp.dot(h_scr[...], wu_ref[:, sl], preferred_element_type=F32)
        a_scr[:, sl] = (_silu(g) * u).astype(BF16)
    y = x_ref[...] + 0.5 * jnp.dot(a_scr[...], wd_ref[...], preferred_element_type=F32)
    if final:
        y = _rms(y, fw_ref[...])
    o_ref[...] = y


def _ffn(x2d, nw, wg, wu, wd, fw=None):
    t, d = x2d.shape
    d_ff = wg.shape[1]
    tm = FFN_TILE
    final = fw is not None
    row = pl.BlockSpec((tm, d), lambda i: (i, 0))
    in_specs = [row, _const_spec((1, d)), _const_spec((d, d_ff)), _const_spec((d, d_ff)),
                _const_spec((d_ff, d))]
    args = [x2d, nw, wg, wu, wd]
    if final:
        in_specs.append(_const_spec((1, d)))
        args.append(fw)
    return pl.pallas_call(
        functools.partial(_ffn_body, final=final),
        grid=(t // tm,),
        in_specs=in_specs,
        out_specs=row,
        out_shape=jax.ShapeDtypeStruct((t, d), F32),
        scratch_shapes=[pltpu.VMEM((tm, d), BF16), pltpu.VMEM((tm, d_ff), BF16)],
        compiler_params=_params(1),
        name="ffn_final" if final else "ffn",
    )(*args)


def _project_inputs(h_scr, w_ref, cw_ref, cos_ref, sin_ref, z_scr, u_scr,
                    yc_scr, rq_scr, rk_scr, rv_scr, rg_scr, aq_scr, k_scr, v_scr):
    tm = h_scr.shape[0]
    bw = BRANCH_W

    def proj(g):
        return jnp.dot(h_scr[...], w_ref[:, g * bw:(g + 1) * bw], preferred_element_type=F32)

    u_scr[...] = proj(0)
    z_scr[8:tm + 8] = proj(2) * u_scr[...]
    conv = (cw_ref[0:1, :] * z_scr[6:tm + 6] + cw_ref[1:2, :] * z_scr[7:tm + 7]
            + cw_ref[2:3, :] * z_scr[8:tm + 8])
    yc_scr[...] = (proj(1) * conv).astype(BF16)

    cos2 = cos_ref[...]
    sin2 = sin_ref[...]
    for g, dst in ((3, rq_scr), (4, rk_scr)):
        r = proj(g)
        for hh in range(H_RET):
            sl = slice(hh * DK_RET, (hh + 1) * DK_RET)
            rh = r[:, sl]
            dst[:, sl] = (rh * cos2 + pltpu.roll(rh, DK_RET // 2, 1) * sin2).astype(BF16)
    rv_scr[...] = proj(5).astype(BF16)
    rg_scr[...] = proj(6).astype(BF16)
    aq_scr[...] = (proj(7) * (DH_ATT ** -0.5)).astype(BF16)
    k_scr[tm:2 * tm] = proj(8).astype(BF16)
    v_scr[tm:2 * tm] = proj(9).astype(BF16)


def _retention_tile(q_scr, k_scr, v_scr, g_scr, wm_ref, qd_ref, kd_ref, cd_ref, st_scr, y_scr):
    blk = wm_ref.shape[1]
    for b in range(q_scr.shape[0] // blk):
        rows = slice(b * blk, (b + 1) * blk)
        for hh in range(H_RET):
            sl = slice(hh * DK_RET, (hh + 1) * DK_RET)
            q = q_scr[rows, sl]
            k = k_scr[rows, sl]
            v = v_scr[rows, sl]
            inner = lax.dot_general(q, k, NT_DIMS, preferred_element_type=F32) * wm_ref[hh]
            st = st_scr[hh]
            o = (jnp.dot(inner.astype(BF16), v, preferred_element_type=F32)
                 + jnp.dot(q, st.astype(BF16), preferred_element_type=F32) * qd_ref[hh])
            kdec = (k.astype(F32) * kd_ref[hh]).astype(BF16)
            st_scr[hh] = st * cd_ref[hh] + lax.dot_general(kdec, v, TN_DIMS,
                                                           preferred_element_type=F32)
            o = o * lax.rsqrt(jnp.mean(o * o, axis=-1, keepdims=True) + EPS)
            y_scr[rows, sl] = (_silu(g_scr[rows, sl].astype(F32)) * o).astype(BF16)


def _attention_tile(tile_idx, q_scr, k_scr, v_scr, bias_ref, y_scr):
    qb = q_scr.shape[0]
    lane_head = lax.broadcasted_iota(jnp.int32, (ATT_SUB, MXU_COLS), 1) // DH_ATT
    col = lax.broadcasted_iota(jnp.int32, (ATT_SUB, ATT_WIN), 1)
    for t in range(qb // ATT_SUB):
        r0 = t * ATT_SUB
        valid = col >= (1 - tile_idx) * qb - r0
        for g in range(H_ATT // HEAD_GROUP):
            cs = slice(g * MXU_COLS, (g + 1) * MXU_COLS)
            qg = q_scr[r0:r0 + ATT_SUB, cs].astype(F32)
            qs = jnp.concatenate(
                [jnp.where(lane_head == hh, qg, 0.0).astype(BF16) for hh in range(HEAD_GROUP)], axis=0)
            sc = lax.dot_general(qs, k_scr[r0:r0 + ATT_WIN, cs], NT_DIMS,
                                 preferred_element_type=F32)
            es, inv = [], []
            for hh in range(HEAD_GROUP):
                sh = sc[hh * ATT_SUB:(hh + 1) * ATT_SUB] + bias_ref[g * HEAD_GROUP + hh]
                sh = jnp.where(valid, sh, NEG_INF)
                e = jnp.exp(sh - jnp.max(sh, axis=-1, keepdims=True))
                inv.append(1.0 / jnp.sum(e, axis=-1, keepdims=True))
                es.append(e.astype(BF16))
            pv = jnp.dot(jnp.concatenate(es, axis=0), v_scr[r0:r0 + ATT_WIN, cs],
                         preferred_element_type=F32)
            og = jnp.zeros((ATT_SUB, MXU_COLS), F32)
            for hh in range(HEAD_GROUP):
                og = jnp.where(lane_head == hh, pv[hh * ATT_SUB:(hh + 1) * ATT_SUB] * inv[hh], og)
            y_scr[r0:r0 + ATT_SUB, cs] = og.astype(BF16)


def _merge_tile(x_ref, h_scr, branches, wg_ref, wb_ref, wo_ref, m_scr, o_ref):
    d = x_ref.shape[1]
    for c in range(d // MXU_COLS):
        sl = slice(c * MXU_COLS, (c + 1) * MXU_COLS)
        acc = None
        for b, y_scr in enumerate(branches):
            gate = jax.nn.sigmoid(jnp.dot(h_scr[...], wg_ref[b, :, sl], preferred_element_type=F32))
            term = gate * jnp.dot(y_scr[...], wb_ref[b, :, sl], preferred_element_type=F32)
            acc = term if acc is None else acc + term
        m_scr[:, sl] = acc.astype(BF16)
    o_ref[...] = x_ref[...] + jnp.dot(m_scr[...], wo_ref[...], preferred_element_type=F32)


def _mixer_body(x_ref, nw_ref, win_ref, cw_ref, cos_ref, sin_ref, wm_ref, qd_ref, kd_ref, cd_ref,
                bias_ref, wg_ref, wb_ref, wo_ref, o_ref,
                h_scr, z_scr, u_scr, yc_scr, yr_scr, ya_scr, rq_scr, rk_scr, rv_scr, rg_scr,
                aq_scr, k_scr, v_scr, st_scr, m_scr):
    tm = x_ref.shape[0]
    j = pl.program_id(1)

    @pl.when(j == 0)
    def _():
        z_scr[0:8] = jnp.zeros((8, BRANCH_W), F32)
        st_scr[...] = jnp.zeros(st_scr.shape, F32)
        k_scr[0:tm] = jnp.zeros((tm, BRANCH_W), BF16)
        v_scr[0:tm] = jnp.zeros((tm, BRANCH_W), BF16)

    @pl.when(j > 0)
    def _():
        z_scr[0:8] = z_scr[tm:tm + 8]
        k_scr[0:tm] = k_scr[tm:2 * tm]
        v_scr[0:tm] = v_scr[tm:2 * tm]

    h_scr[...] = _rms(x_ref[...], nw_ref[...]).astype(BF16)
    _project_inputs(h_scr, win_ref, cw_ref, cos_ref, sin_ref, z_scr, u_scr,
                    yc_scr, rq_scr, rk_scr, rv_scr, rg_scr, aq_scr, k_scr, v_scr)
    _retention_tile(rq_scr, rk_scr, rv_scr, rg_scr, wm_ref, qd_ref, kd_ref, cd_ref, st_scr, yr_scr)
    _attention_tile(j, aq_scr, k_scr, v_scr, bias_ref, ya_scr)
    _merge_tile(x_ref, h_scr, (yc_scr, yr_scr, ya_scr), wg_ref, wb_ref, wo_ref, m_scr, o_ref)


def _retention_tables(blk):
    log_gamma = jnp.log1p(-jnp.exp2(-5.0 - jnp.arange(H_RET, dtype=F32)))
    pos = jnp.arange(blk, dtype=F32)
    chunk = jnp.arange(blk) // CHUNK
    scale = DK_RET ** -0.5
    decay = jnp.exp(log_gamma[:, None, None] * jnp.abs(pos[:, None] - pos[None, :]))
    wm = jnp.where(chunk[None, :, None] >= chunk[None, None, :], decay, 0.0) * scale
    qd = jnp.exp(log_gamma[:, None] * (pos + 1.0)) * scale
    kd = jnp.exp(log_gamma[:, None] * (blk - 1.0 - pos))
    cd = jnp.exp(log_gamma * blk)
    bc = lambda a: jnp.broadcast_to(a[..., None], a.shape + (DK_RET,))
    return wm, bc(qd), bc(kd), bc(cd)[:, None, :]


def _attention_bias(rel_bias):
    h = rel_bias.shape[0]
    back = N_PREV_CHUNKS * CHUNK
    period = ATT_WIN + ATT_SUB
    far = rel_bias[:, -1:].astype(F32)
    row = jnp.concatenate([jnp.broadcast_to(far, (h, back - REL_CLIP)),
                           rel_bias[:, ::-1].astype(F32),
                           jnp.broadcast_to(far, (h, period - back - REL_CLIP - 1))], axis=1)
    flat = jnp.broadcast_to(row[:, None, :], (h, ATT_SUB, period)).reshape(h, ATT_SUB * period)
    skew = flat[:, :ATT_SUB * (period - 1)].reshape(h, ATT_SUB, period - 1)[:, :, :ATT_WIN]
    qc = (jnp.arange(ATT_SUB) // CHUNK)[:, None]
    kc = (jnp.arange(ATT_WIN) // CHUNK)[None, :]
    band = (kc >= qc) & (kc <= qc + N_PREV_CHUNKS)
    return jnp.where(band[None], skew, NEG_INF)


def _mixer(x2d, bsz, nw, w_in, conv_w, cos2, sin2, ret_tables, bias, w_gate, w_branch, w_out):
    t, d = x2d.shape
    s = t // bsz
    tm = MIX_TILE
    nt = s // tm
    bw = BRANCH_W
    row = pl.BlockSpec((tm, d), lambda b, j: (b * nt + j, 0))
    pos = pl.BlockSpec((tm, DK_RET), lambda b, j: (j, 0))
    consts = [w_in, conv_w, None, None, *ret_tables, bias, w_gate, w_branch, w_out]
    in_specs = [row, _const_spec((1, d))] + [pos if c is None else _const_spec(c.shape) for c in consts]
    tile = lambda dtype: pltpu.VMEM((tm, bw), dtype)
    scratch = [pltpu.VMEM((tm, d), BF16),
               pltpu.VMEM((tm + 8, bw), F32),
               tile(F32),
               tile(BF16), tile(BF16), tile(BF16),
               tile(BF16), tile(BF16), tile(BF16), tile(BF16),
               tile(BF16),
               pltpu.VMEM((2 * tm, bw), BF16), pltpu.VMEM((2 * tm, bw), BF16),
               pltpu.VMEM((H_RET, DK_RET, DK_RET), F32),
               pltpu.VMEM((tm, d), BF16)]
    return pl.pallas_call(
        _mixer_body,
        grid=(bsz, nt),
        in_specs=in_specs,
        out_specs=row,
        out_shape=jax.ShapeDtypeStruct((t, d), F32),
        scratch_shapes=scratch,
        compiler_params=_params(2),
        name="mixer",
    )(x2d, nw, w_in, conv_w, cos2, sin2, *ret_tables, bias, w_gate, w_branch, w_out)


def kernel(x, ffn1_norm, ffn1_w_gate, ffn1_w_up, ffn1_w_down, mix_norm, w_in, conv_w, rel_bias,
           w_branch, w_merge_gate, w_out, ffn2_norm, ffn2_w_gate, ffn2_w_up, ffn2_w_down, final_norm):
    bsz, s, d = x.shape
    depth = w_in.shape[0]
    assert (bsz * s) % FFN_TILE == 0 and s % MIX_TILE == 0
    assert MIX_TILE % RET_BLOCK == 0 and RET_BLOCK % CHUNK == 0 and MIX_TILE % ATT_SUB == 0

    inv_freq = ROPE_BASE ** (-jnp.linspace(0.0, 1.0, DK_RET // 2, dtype=F32))
    ang = jnp.arange(s, dtype=F32)[:, None] * inv_freq[None, :]
    cos, sin = jnp.cos(ang), jnp.sin(ang)
    cos2 = jnp.concatenate([cos, cos], axis=-1)
    sin2 = jnp.concatenate([-sin, sin], axis=-1)
    ret_tables = _retention_tables(RET_BLOCK)

    vec = lambda w: w.reshape(1, d).astype(F32)
    bf = lambda w: w.astype(BF16)
    xf = x.reshape(bsz * s, d)
    for l in range(depth):
        xf = _ffn(xf, vec(ffn1_norm[l]), bf(ffn1_w_gate[l]), bf(ffn1_w_up[l]), bf(ffn1_w_down[l]))
        xf = _mixer(xf, bsz, vec(mix_norm[l]), bf(w_in[l]), conv_w[l].astype(F32), cos2, sin2,
                    ret_tables, _attention_bias(rel_bias[l]), bf(w_merge_gate[l]), bf(w_branch[l]),
                    bf(w_out[l]))
        xf = _ffn(xf, vec(ffn2_norm[l]), bf(ffn2_w_gate[l]), bf(ffn2_w_up[l]), bf(ffn2_w_down[l]),
                  fw=vec(final_norm) if l == depth - 1 else None)
    return xf.reshape(bsz, s, d)
```

```python
import functools

import jax
import jax.numpy as jnp
from jax import lax
from jax.experimental import pallas as pl
from jax.experimental.pallas import tpu as pltpu

F32 = jnp.float32
BF16 = jnp.bfloat16

CHUNK = 64
BRANCH_W = 512
H_RET = 4
DK_RET = BRANCH_W // H_RET
H_ATT = 8
DH_ATT = BRANCH_W // H_ATT
N_PREV_CHUNKS = 8
REL_CLIP = 128
EPS = 1e-6
NEG_INF = -1e30
ROPE_BASE = 10000.0

MXU_COLS = 256
LANES = 128
BF16_SUBLANES = 16
MIB = 1024 * 1024
FFN_VMEM_BYTES = 50 * MIB
MIXER_VMEM_BYTES = 56 * MIB

FFN_TILE = 1024
FF_CHUNK = 256
MIX_TILE = N_PREV_CHUNKS * CHUNK
RET_BLOCK = 256
ATT_SUB = 128
ATT_WIN = ATT_SUB + N_PREV_CHUNKS * CHUNK
HEAD_GROUP = MXU_COLS // DH_ATT

NT_DIMS = (((1,), (1,)), ((), ()))
TN_DIMS = (((0,), (0,)), ((), ()))


def _const_spec(shape):
    zeros = (0,) * len(shape)
    return pl.BlockSpec(shape, lambda *_: zeros, pipeline_mode=pl.Buffered(1))


def _params(n_axes, vmem_bytes):
    return pltpu.CompilerParams(dimension_semantics=("arbitrary",) * n_axes,
                                vmem_limit_bytes=vmem_bytes)


def _stage_specs(weights, n_steps, step_of):
    in_specs, out_specs, out_shapes = [], [], []
    for w, layer in weights:
        _, rows, cols = w.shape
        for n_col in (1, 2, 4, 8, 16, 32, 64, 128):
            n_row = n_steps // n_col
            if n_steps % n_col or rows % n_row or cols % n_col:
                continue
            br, bc = rows // n_row, cols // n_col
            if br % BF16_SUBLANES == 0 and bc % LANES == 0:
                break
        else:
            raise ValueError(f"no {n_steps}-block tiling for weight {w.shape}")
        in_specs.append(pl.BlockSpec(
            (None, br, bc),
            lambda *g, layer=layer, n_col=n_col: (layer, step_of(*g) // n_col, step_of(*g) % n_col)))
        out_specs.append(pl.BlockSpec(
            (br, bc), lambda *g, n_col=n_col: (step_of(*g) // n_col, step_of(*g) % n_col)))
        out_shapes.append(jax.ShapeDtypeStruct((rows, cols), BF16))
    return in_specs, out_specs, out_shapes


def _cast_staged(src_refs, dst_refs):
    for src, dst in zip(src_refs, dst_refs):
        dst[...] = src[...].astype(BF16)


def _rms(x, w):
    ms = jnp.mean(x * x, axis=-1, keepdims=True)
    return x * lax.rsqrt(ms + EPS) * w


def _silu(g):
    return g * jax.nn.sigmoid(g)


def _ffn_body(*refs, final, n_stage):
    n_in = 6 if final else 5
    x_ref, nw_ref, wg_ref, wu_ref, wd_ref = refs[:5]
    fw_ref = refs[5] if final else None
    stage_in = refs[n_in:n_in + n_stage]
    o_ref = refs[n_in + n_stage]
    stage_out = refs[n_in + n_stage + 1:n_in + 2 * n_stage + 1]
    h_scr, a_scr = refs[n_in + 2 * n_stage + 1:]
    _cast_staged(stage_in, stage_out)
    h_scr[...] = _rms(x_ref[...], nw_ref[...]).astype(BF16)
    d_ff = wg_ref.shape[1]
    for c in range(d_ff // FF_CHUNK):
        sl = slice(c * FF_CHUNK, (c + 1) * FF_CHUNK)
        g = jnp.dot(h_scr[...], wg_ref[:, sl], preferred_element_type=F32)
        u = jnp.dot(h_scr[...], wu_ref[:, sl], preferred_element_type=F32)
        a_scr[:, sl] = (_silu(g) * u).astype(BF16)
    y = x_ref[...] + 0.5 * jnp.dot(a_scr[...], wd_ref[...], preferred_element_type=F32)
    if final:
        y = _rms(y, fw_ref[...])
    o_ref[...] = y


def _ffn(x2d, nw, wg, wu, wd, fw=None, stage=()):
    t, d = x2d.shape
    d_ff = wg.shape[1]
    tm = FFN_TILE
    final = fw is not None
    row = pl.BlockSpec((tm, d), lambda i: (i, 0))
    in_specs = [row, _const_spec((1, d)), _const_spec((d, d_ff)), _const_spec((d, d_ff)),
                _const_spec((d_ff, d))]
    args = [x2d, nw, wg, wu, wd]
    if final:
        in_specs.append(_const_spec((1, d)))
        args.append(fw)
    st_in, st_out, st_shapes = _stage_specs(stage, t // tm, lambda i: i)
    out = pl.pallas_call(
        functools.partial(_ffn_body, final=final, n_stage=len(stage)),
        grid=(t // tm,),
        in_specs=in_specs + st_in,
        out_specs=[row] + st_out,
        out_shape=[jax.ShapeDtypeStruct((t, d), F32)] + st_shapes,
        scratch_shapes=[pltpu.VMEM((tm, d), BF16), pltpu.VMEM((tm, d_ff), BF16)],
        compiler_params=_params(1, FFN_VMEM_BYTES),
        name="ffn_final" if final else "ffn",
    )(*args, *[w for w, _ in stage])
    return out[0], out[1:]


def _project_inputs(h_scr, w_ref, cw_ref, cos_ref, sin_ref, z_scr, u_scr,
                    yc_scr, rq_scr, rk_scr, rv_scr, rg_scr, aq_scr, k_scr, v_scr):
    tm = h_scr.shape[0]
    bw = BRANCH_W

    def proj(g):
        return jnp.dot(h_scr[...], w_ref[:, g * bw:(g + 1) * bw], preferred_element_type=F32)

    u_scr[...] = proj(0)
    z_scr[8:tm + 8] = proj(2) * u_scr[...]
    conv = (cw_ref[0:1, :] * z_scr[6:tm + 6] + cw_ref[1:2, :] * z_scr[7:tm + 7]
            + cw_ref[2:3, :] * z_scr[8:tm + 8])
    yc_scr[...] = (proj(1) * conv).astype(BF16)

    cos2 = cos_ref[...]
    sin2 = sin_ref[...]
    for g, dst in ((3, rq_scr), (4, rk_scr)):
        r = proj(g)
        for hh in range(H_RET):
            sl = slice(hh * DK_RET, (hh + 1) * DK_RET)
            rh = r[:, sl]
            dst[:, sl] = (rh * cos2 + pltpu.roll(rh, DK_RET // 2, 1) * sin2).astype(BF16)
    rv_scr[...] = proj(5).astype(BF16)
    rg_scr[...] = proj(6).astype(BF16)
    aq_scr[...] = (proj(7) * (DH_ATT ** -0.5)).astype(BF16)
    k_scr[tm:2 * tm] = proj(8).astype(BF16)
    v_scr[tm:2 * tm] = proj(9).astype(BF16)


def _retention_tile(q_scr, k_scr, v_scr, g_scr, wm_ref, qd_ref, kd_ref, cd_ref, st_scr, y_scr):
    blk = wm_ref.shape[1]
    for b in range(q_scr.shape[0] // blk):
        rows = slice(b * blk, (b + 1) * blk)
        for hh in range(H_RET):
            sl = slice(hh * DK_RET, (hh + 1) * DK_RET)
            q = q_scr[rows, sl]
            k = k_scr[rows, sl]
            v = v_scr[rows, sl]
            inner = lax.dot_general(q, k, NT_DIMS, preferred_element_type=F32) * wm_ref[hh]
            st = st_scr[hh]
            o = (jnp.dot(inner.astype(BF16), v, preferred_element_type=F32)
                 + jnp.dot(q, st.astype(BF16), preferred_element_type=F32) * qd_ref[hh])
            kdec = (k.astype(F32) * kd_ref[hh]).astype(BF16)
            st_scr[hh] = st * cd_ref[hh] + lax.dot_general(kdec, v, TN_DIMS,
                                                           preferred_element_type=F32)
            o = o * lax.rsqrt(jnp.mean(o * o, axis=-1, keepdims=True) + EPS)
            y_scr[rows, sl] = (_silu(g_scr[rows, sl].astype(F32)) * o).astype(BF16)


def _attention_tile(tile_idx, q_scr, k_scr, v_scr, bias_ref, y_scr):
    qb = q_scr.shape[0]
    lane_head = lax.broadcasted_iota(jnp.int32, (ATT_SUB, MXU_COLS), 1) // DH_ATT
    col = lax.broadcasted_iota(jnp.int32, (ATT_SUB, ATT_WIN), 1)
    for t in range(qb // ATT_SUB):
        r0 = t * ATT_SUB
        valid = col >= (1 - tile_idx) * qb - r0
        for g in range(H_ATT // HEAD_GROUP):
            cs = slice(g * MXU_COLS, (g + 1) * MXU_COLS)
            qg = q_scr[r0:r0 + ATT_SUB, cs].astype(F32)
            qs = jnp.concatenate(
                [jnp.where(lane_head == hh, qg, 0.0).astype(BF16) for hh in range(HEAD_GROUP)], axis=0)
            sc = lax.dot_general(qs, k_scr[r0:r0 + ATT_WIN, cs], NT_DIMS,
                                 preferred_element_type=F32)
            es, inv = [], []
            for hh in range(HEAD_GROUP):
                sh = sc[hh * ATT_SUB:(hh + 1) * ATT_SUB] + bias_ref[g * HEAD_GROUP + hh]
                sh = jnp.where(valid, sh, NEG_INF)
                e = jnp.exp(sh - jnp.max(sh, axis=-1, keepdims=True))
                inv.append(1.0 / jnp.sum(e, axis=-1, keepdims=True))
                es.append(e.astype(BF16))
            pv = jnp.dot(jnp.concatenate(es, axis=0), v_scr[r0:r0 + ATT_WIN, cs],
                         preferred_element_type=F32)
            og = jnp.zeros((ATT_SUB, MXU_COLS), F32)
            for hh in range(HEAD_GROUP):
                og = jnp.where(lane_head == hh, pv[hh * ATT_SUB:(hh + 1) * ATT_SUB] * inv[hh], og)
            y_scr[r0:r0 + ATT_SUB, cs] = og.astype(BF16)


def _merge_tile(x_ref, h_scr, branches, wg_ref, wb_ref, wo_ref, m_scr, o_ref):
    d = x_ref.shape[1]
    bw = BRANCH_W
    for c in range(d // MXU_COLS):
        sl = slice(c * MXU_COLS, (c + 1) * MXU_COLS)
        acc = None
        for b, y_scr in enumerate(branches):
            gate = jax.nn.sigmoid(jnp.dot(h_scr[...], wg_ref[b * d:(b + 1) * d, sl],
                                          preferred_element_type=F32))
            term = gate * jnp.dot(y_scr[...], wb_ref[b * bw:(b + 1) * bw, sl],
                                  preferred_element_type=F32)
            acc = term if acc is None else acc + term
        m_scr[:, sl] = acc.astype(BF16)
    o_ref[...] = x_ref[...] + jnp.dot(m_scr[...], wo_ref[...], preferred_element_type=F32)


def _mixer_body(*refs, n_stage):
    n_in = 14
    (x_ref, nw_ref, win_ref, cw_ref, cos_ref, sin_ref, wm_ref, qd_ref, kd_ref, cd_ref,
     bias_ref, wg_ref, wb_ref, wo_ref) = refs[:n_in]
    stage_in = refs[n_in:n_in + n_stage]
    o_ref = refs[n_in + n_stage]
    stage_out = refs[n_in + n_stage + 1:n_in + 2 * n_stage + 1]
    (h_scr, z_scr, u_scr, yc_scr, yr_scr, ya_scr, rq_scr, rk_scr, rv_scr, rg_scr,
     aq_scr, k_scr, v_scr, st_scr, m_scr) = refs[n_in + 2 * n_stage + 1:]
    tm = x_ref.shape[0]
    j = pl.program_id(1)

    @pl.when(j == 0)
    def _():
        z_scr[0:8] = jnp.zeros((8, BRANCH_W), F32)
        st_scr[...] = jnp.zeros(st_scr.shape, F32)
        k_scr[0:tm] = jnp.zeros((tm, BRANCH_W), BF16)
        v_scr[0:tm] = jnp.zeros((tm, BRANCH_W), BF16)

    @pl.when(j > 0)
    def _():
        z_scr[0:8] = z_scr[tm:tm + 8]
        k_scr[0:tm] = k_scr[tm:2 * tm]
        v_scr[0:tm] = v_scr[tm:2 * tm]

    _cast_staged(stage_in, stage_out)
    h_scr[...] = _rms(x_ref[...], nw_ref[...]).astype(BF16)
    _project_inputs(h_scr, win_ref, cw_ref, cos_ref, sin_ref, z_scr, u_scr,
                    yc_scr, rq_scr, rk_scr, rv_scr, rg_scr, aq_scr, k_scr, v_scr)
    _retention_tile(rq_scr, rk_scr, rv_scr, rg_scr, wm_ref, qd_ref, kd_ref, cd_ref, st_scr, yr_scr)
    _attention_tile(j, aq_scr, k_scr, v_scr, bias_ref, ya_scr)
    _merge_tile(x_ref, h_scr, (yc_scr, yr_scr, ya_scr), wg_ref, wb_ref, wo_ref, m_scr, o_ref)


def _retention_tables(blk):
    log_gamma = jnp.log1p(-jnp.exp2(-5.0 - jnp.arange(H_RET, dtype=F32)))
    pos = jnp.arange(blk, dtype=F32)
    chunk = jnp.arange(blk) // CHUNK
    scale = DK_RET ** -0.5
    decay = jnp.exp(log_gamma[:, None, None] * jnp.abs(pos[:, None] - pos[None, :]))
    wm = jnp.where(chunk[None, :, None] >= chunk[None, None, :], decay, 0.0) * scale
    qd = jnp.exp(log_gamma[:, None] * (pos + 1.0)) * scale
    kd = jnp.exp(log_gamma[:, None] * (blk - 1.0 - pos))
    cd = jnp.exp(log_gamma * blk)
    bc = lambda a: jnp.broadcast_to(a[..., None], a.shape + (DK_RET,))
    return wm, bc(qd), bc(kd), bc(cd)[:, None, :]


def _attention_bias(rel_bias):
    h = rel_bias.shape[0]
    back = N_PREV_CHUNKS * CHUNK
    period = ATT_WIN + ATT_SUB
    far = rel_bias[:, -1:].astype(F32)
    row = jnp.concatenate([jnp.broadcast_to(far, (h, back - REL_CLIP)),
                           rel_bias[:, ::-1].astype(F32),
                           jnp.broadcast_to(far, (h, period - back - REL_CLIP - 1))], axis=1)
    flat = jnp.broadcast_to(row[:, None, :], (h, ATT_SUB, period)).reshape(h, ATT_SUB * period)
    skew = flat[:, :ATT_SUB * (period - 1)].reshape(h, ATT_SUB, period - 1)[:, :, :ATT_WIN]
    qc = (jnp.arange(ATT_SUB) // CHUNK)[:, None]
    kc = (jnp.arange(ATT_WIN) // CHUNK)[None, :]
    band = (kc >= qc) & (kc <= qc + N_PREV_CHUNKS)
    return jnp.where(band[None], skew, NEG_INF)


def _mixer(x2d, bsz, nw, w_in, conv_w, cos2, sin2, ret_tables, bias, w_gate, w_branch, w_out,
           stage=()):
    t, d = x2d.shape
    s = t // bsz
    tm = MIX_TILE
    nt = s // tm
    bw = BRANCH_W
    row = pl.BlockSpec((tm, d), lambda b, j: (b * nt + j, 0))
    pos = pl.BlockSpec((tm, DK_RET), lambda b, j: (j, 0))
    consts = [w_in, conv_w, None, None, *ret_tables, bias, w_gate, w_branch, w_out]
    in_specs = [row, _const_spec((1, d))] + [pos if c is None else _const_spec(c.shape) for c in consts]
    tile = lambda dtype: pltpu.VMEM((tm, bw), dtype)
    scratch = [pltpu.VMEM((tm, d), BF16),
               pltpu.VMEM((tm + 8, bw), F32),
               tile(F32),
               tile(BF16), tile(BF16), tile(BF16),
               tile(BF16), tile(BF16), tile(BF16), tile(BF16),
               tile(BF16),
               pltpu.VMEM((2 * tm, bw), BF16), pltpu.VMEM((2 * tm, bw), BF16),
               pltpu.VMEM((H_RET, DK_RET, DK_RET), F32),
               pltpu.VMEM((tm, d), BF16)]
    st_in, st_out, st_shapes = _stage_specs(stage, bsz * nt, lambda b, j: b * nt + j)
    out = pl.pallas_call(
        functools.partial(_mixer_body, n_stage=len(stage)),
        grid=(bsz, nt),
        in_specs=in_specs + st_in,
        out_specs=[row] + st_out,
        out_shape=[jax.ShapeDtypeStruct((t, d), F32)] + st_shapes,
        scratch_shapes=scratch,
        compiler_params=_params(2, MIXER_VMEM_BYTES),
        name="mixer",
    )(x2d, nw, w_in, conv_w, cos2, sin2, *ret_tables, bias, w_gate, w_branch, w_out,
      *[w for w, _ in stage])
    return out[0], out[1:]


def kernel(x, ffn1_norm, ffn1_w_gate, ffn1_w_up, ffn1_w_down, mix_norm, w_in, conv_w, rel_bias,
           w_branch, w_merge_gate, w_out, ffn2_norm, ffn2_w_gate, ffn2_w_up, ffn2_w_down, final_norm):
    bsz, s, d = x.shape
    depth = w_in.shape[0]
    assert (bsz * s) % FFN_TILE == 0 and s % MIX_TILE == 0
    assert MIX_TILE % RET_BLOCK == 0 and RET_BLOCK % CHUNK == 0 and MIX_TILE % ATT_SUB == 0

    inv_freq = ROPE_BASE ** (-jnp.linspace(0.0, 1.0, DK_RET // 2, dtype=F32))
    ang = jnp.arange(s, dtype=F32)[:, None] * inv_freq[None, :]
    cos, sin = jnp.cos(ang), jnp.sin(ang)
    cos2 = jnp.concatenate([cos, cos], axis=-1)
    sin2 = jnp.concatenate([-sin, sin], axis=-1)
    ret_tables = _retention_tables(RET_BLOCK)

    vec = lambda w: w.reshape(1, d).astype(F32)
    gate_rows = w_merge_gate.reshape(depth, -1, d)
    branch_rows = w_branch.reshape(depth, -1, d)
    mixer_weights = lambda l: [(w_in, l), (gate_rows, l), (branch_rows, l), (w_out, l)]
    ffn1_weights = lambda l: [(ffn1_w_gate, l), (ffn1_w_up, l), (ffn1_w_down, l)]
    ffn2_weights = lambda l: [(ffn2_w_gate, l), (ffn2_w_up, l), (ffn2_w_down, l)]

    xf = x.reshape(bsz * s, d)
    ffn_w = [w[0].astype(BF16) for w, _ in ffn1_weights(0)]
    for l in range(depth):
        last = l == depth - 1
        xf, staged = _ffn(xf, vec(ffn1_norm[l]), *ffn_w, stage=mixer_weights(l) + ffn2_weights(l))
        w_in_b, w_gate_b, w_branch_b, w_out_b = staged[:4]
        ffn_w = staged[4:]
        xf, _ = _mixer(xf, bsz, vec(mix_norm[l]), w_in_b, conv_w[l].astype(F32), cos2, sin2,
                       ret_tables, _attention_bias(rel_bias[l]), w_gate_b, w_branch_b, w_out_b)
        xf, ffn_w = _ffn(xf, vec(ffn2_norm[l]), *ffn_w, fw=vec(final_norm) if last else None,
                         stage=() if last else ffn1_weights(l + 1))
    return xf.reshape(bsz, s, d)
```

```python
import functools

import jax
import jax.numpy as jnp
from jax import lax
from jax.experimental import pallas as pl
from jax.experimental.pallas import tpu as pltpu

F32 = jnp.float32
BF16 = jnp.bfloat16

CHUNK = 64
BRANCH_W = 512
H_RET = 4
DK_RET = BRANCH_W // H_RET
H_ATT = 8
DH_ATT = BRANCH_W // H_ATT
N_PREV_CHUNKS = 8
REL_CLIP = 128
EPS = 1e-6
NEG_INF = -1e30
ROPE_BASE = 10000.0

MXU_COLS = 256
LANES = 128
BF16_SUBLANES = 16
MIB = 1024 * 1024
FFN_VMEM_BYTES = 50 * MIB
MIXER_VMEM_BYTES = 56 * MIB

FFN_TILE = 1024
FFN_SUB = 256
FF_CHUNK = 256
MIX_TILE = N_PREV_CHUNKS * CHUNK
RET_BLOCK = 256
ATT_SUB = 128
ATT_WIN = ATT_SUB + N_PREV_CHUNKS * CHUNK
HEAD_GROUP = MXU_COLS // DH_ATT

NT_DIMS = (((1,), (1,)), ((), ()))
TN_DIMS = (((0,), (0,)), ((), ()))


def _const_spec(shape):
    zeros = (0,) * len(shape)
    return pl.BlockSpec(shape, lambda *_: zeros, pipeline_mode=pl.Buffered(1))


def _params(n_axes, vmem_bytes):
    return pltpu.CompilerParams(dimension_semantics=("arbitrary",) * n_axes,
                                vmem_limit_bytes=vmem_bytes)


def _stage_specs(weights, n_steps, step_of):
    in_specs, out_specs, out_shapes = [], [], []
    for w, layer in weights:
        _, rows, cols = w.shape
        for n_col in (1, 2, 4, 8, 16, 32, 64, 128):
            n_row = n_steps // n_col
            if n_steps % n_col or rows % n_row or cols % n_col:
                continue
            br, bc = rows // n_row, cols // n_col
            if br % BF16_SUBLANES == 0 and bc % LANES == 0:
                break
        else:
            raise ValueError(f"no {n_steps}-block tiling for weight {w.shape}")
        in_specs.append(pl.BlockSpec(
            (None, br, bc),
            lambda *g, layer=layer, n_col=n_col: (layer, step_of(*g) // n_col, step_of(*g) % n_col)))
        out_specs.append(pl.BlockSpec(
            (br, bc), lambda *g, n_col=n_col: (step_of(*g) // n_col, step_of(*g) % n_col)))
        out_shapes.append(jax.ShapeDtypeStruct((rows, cols), BF16))
    return in_specs, out_specs, out_shapes


def _cast_staged(src_refs, dst_refs):
    for src, dst in zip(src_refs, dst_refs):
        dst[...] = src[...].astype(BF16)


def _rms(x, w):
    ms = jnp.mean(x * x, axis=-1, keepdims=True)
    return x * lax.rsqrt(ms + EPS) * w


def _silu(g):
    return g * jax.nn.sigmoid(g)


def _ffn_body(*refs, final, n_stage):
    n_in = 6 if final else 5
    x_ref, nw_ref, wg_ref, wu_ref, wd_ref = refs[:5]
    fw_ref = refs[5] if final else None
    stage_in = refs[n_in:n_in + n_stage]
    o_ref = refs[n_in + n_stage]
    stage_out = refs[n_in + n_stage + 1:n_in + 2 * n_stage + 1]
    h_scr, a_scr = refs[n_in + 2 * n_stage + 1:]
    _cast_staged(stage_in, stage_out)
    d_ff = wg_ref.shape[1]
    for r in range(x_ref.shape[0] // FFN_SUB):
        rows = slice(r * FFN_SUB, (r + 1) * FFN_SUB)
        h_scr[rows] = _rms(x_ref[rows], nw_ref[...]).astype(BF16)
        for c in range(d_ff // FF_CHUNK):
            sl = slice(c * FF_CHUNK, (c + 1) * FF_CHUNK)
            g = jnp.dot(h_scr[rows], wg_ref[:, sl], preferred_element_type=F32)
            u = jnp.dot(h_scr[rows], wu_ref[:, sl], preferred_element_type=F32)
            a_scr[rows, sl] = (_silu(g) * u).astype(BF16)
        y = x_ref[rows] + 0.5 * jnp.dot(a_scr[rows], wd_ref[...], preferred_element_type=F32)
        if final:
            y = _rms(y, fw_ref[...])
        o_ref[rows] = y


def _ffn(x2d, nw, wg, wu, wd, fw=None, stage=()):
    t, d = x2d.shape
    d_ff = wg.shape[1]
    tm = FFN_TILE
    final = fw is not None
    row = pl.BlockSpec((tm, d), lambda i: (i, 0))
    in_specs = [row, _const_spec((1, d)), _const_spec((d, d_ff)), _const_spec((d, d_ff)),
                _const_spec((d_ff, d))]
    args = [x2d, nw, wg, wu, wd]
    if final:
        in_specs.append(_const_spec((1, d)))
        args.append(fw)
    st_in, st_out, st_shapes = _stage_specs(stage, t // tm, lambda i: i)
    out = pl.pallas_call(
        functools.partial(_ffn_body, final=final, n_stage=len(stage)),
        grid=(t // tm,),
        in_specs=in_specs + st_in,
        out_specs=[row] + st_out,
        out_shape=[jax.ShapeDtypeStruct((t, d), F32)] + st_shapes,
        scratch_shapes=[pltpu.VMEM((tm, d), BF16), pltpu.VMEM((tm, d_ff), BF16)],
        compiler_params=_params(1, FFN_VMEM_BYTES),
        name="ffn_final" if final else "ffn",
    )(*args, *[w for w, _ in stage])
    return out[0], out[1:]


def _project_inputs(h_scr, w_ref, cw_ref, cos_ref, sin_ref, z_scr, u_scr,
                    yc_scr, rq_scr, rk_scr, rv_scr, rg_scr, aq_scr, k_scr, v_scr):
    tm = h_scr.shape[0]
    bw = BRANCH_W

    def proj(g):
        return jnp.dot(h_scr[...], w_ref[:, g * bw:(g + 1) * bw], preferred_element_type=F32)

    u_scr[...] = proj(0)
    z_scr[8:tm + 8] = proj(2) * u_scr[...]
    conv = (cw_ref[0:1, :] * z_scr[6:tm + 6] + cw_ref[1:2, :] * z_scr[7:tm + 7]
            + cw_ref[2:3, :] * z_scr[8:tm + 8])
    yc_scr[...] = (proj(1) * conv).astype(BF16)

    cos2 = cos_ref[...]
    sin2 = sin_ref[...]
    for g, dst in ((3, rq_scr), (4, rk_scr)):
        r = proj(g)
        for hh in range(H_RET):
            sl = slice(hh * DK_RET, (hh + 1) * DK_RET)
            rh = r[:, sl]
            dst[:, sl] = (rh * cos2 + pltpu.roll(rh, DK_RET // 2, 1) * sin2).astype(BF16)
    rv_scr[...] = proj(5).astype(BF16)
    rg_scr[...] = proj(6).astype(BF16)
    aq_scr[...] = (proj(7) * (DH_ATT ** -0.5)).astype(BF16)
    k_scr[tm:2 * tm] = proj(8).astype(BF16)
    v_scr[tm:2 * tm] = proj(9).astype(BF16)


def _retention_tile(q_scr, k_scr, v_scr, g_scr, wm_ref, qd_ref, kd_ref, cd_ref, st_scr, y_scr):
    blk = wm_ref.shape[1]
    heads = range(H_RET)
    sls = [slice(hh * DK_RET, (hh + 1) * DK_RET) for hh in heads]
    for b in range(q_scr.shape[0] // blk):
        rows = slice(b * blk, (b + 1) * blk)
        q = [q_scr[rows, sl] for sl in sls]
        k = [k_scr[rows, sl] for sl in sls]
        v = [v_scr[rows, sl] for sl in sls]
        st = [st_scr[hh] for hh in heads]
        inner = [(lax.dot_general(q[hh], k[hh], NT_DIMS, preferred_element_type=F32)
                  * wm_ref[hh]).astype(BF16) for hh in heads]
        cross = [jnp.dot(q[hh], st[hh].astype(BF16), preferred_element_type=F32) * qd_ref[hh]
                 for hh in heads]
        for hh in heads:
            kdec = (k[hh].astype(F32) * kd_ref[hh]).astype(BF16)
            st_scr[hh] = st[hh] * cd_ref[hh] + lax.dot_general(kdec, v[hh], TN_DIMS,
                                                               preferred_element_type=F32)
        for hh in heads:
            o = jnp.dot(inner[hh], v[hh], preferred_element_type=F32) + cross[hh]
            o = o * lax.rsqrt(jnp.mean(o * o, axis=-1, keepdims=True) + EPS)
            y_scr[rows, sls[hh]] = (_silu(g_scr[rows, sls[hh]].astype(F32)) * o).astype(BF16)


def _attention_tile(tile_idx, q_scr, k_scr, v_scr, bias_ref, y_scr):
    qb = q_scr.shape[0]
    lane_head = lax.broadcasted_iota(jnp.int32, (ATT_SUB, MXU_COLS), 1) // DH_ATT
    col = lax.broadcasted_iota(jnp.int32, (ATT_SUB, ATT_WIN), 1)
    for t in range(qb // ATT_SUB):
        r0 = t * ATT_SUB
        valid = col >= (1 - tile_idx) * qb - r0
        for g in range(H_ATT // HEAD_GROUP):
            cs = slice(g * MXU_COLS, (g + 1) * MXU_COLS)
            qg = q_scr[r0:r0 + ATT_SUB, cs].astype(F32)
            qs = jnp.concatenate(
                [jnp.where(lane_head == hh, qg, 0.0).astype(BF16) for hh in range(HEAD_GROUP)], axis=0)
            sc = lax.dot_general(qs, k_scr[r0:r0 + ATT_WIN, cs], NT_DIMS,
                                 preferred_element_type=F32)
            es, inv = [], []
            for hh in range(HEAD_GROUP):
                sh = sc[hh * ATT_SUB:(hh + 1) * ATT_SUB] + bias_ref[g * HEAD_GROUP + hh]
                sh = jnp.where(valid, sh, NEG_INF)
                e = jnp.exp(sh - jnp.max(sh, axis=-1, keepdims=True))
                inv.append(1.0 / jnp.sum(e, axis=-1, keepdims=True))
                es.append(e.astype(BF16))
            pv = jnp.dot(jnp.concatenate(es, axis=0), v_scr[r0:r0 + ATT_WIN, cs],
                         preferred_element_type=F32)
            og = jnp.zeros((ATT_SUB, MXU_COLS), F32)
            for hh in range(HEAD_GROUP):
                og = jnp.where(lane_head == hh, pv[hh * ATT_SUB:(hh + 1) * ATT_SUB] * inv[hh], og)
            y_scr[r0:r0 + ATT_SUB, cs] = og.astype(BF16)


def _merge_tile(x_ref, h_scr, branches, wg_ref, wb_ref, wo_ref, m_scr, o_ref):
    d = x_ref.shape[1]
    bw = BRANCH_W
    for c in range(d // MXU_COLS):
        sl = slice(c * MXU_COLS, (c + 1) * MXU_COLS)
        acc = None
        for b, y_scr in enumerate(branches):
            gate = jax.nn.sigmoid(jnp.dot(h_scr[...], wg_ref[b * d:(b + 1) * d, sl],
                                          preferred_element_type=F32))
            term = gate * jnp.dot(y_scr[...], wb_ref[b * bw:(b + 1) * bw, sl],
                                  preferred_element_type=F32)
            acc = term if acc is None else acc + term
        m_scr[:, sl] = acc.astype(BF16)
    o_ref[...] = x_ref[...] + jnp.dot(m_scr[...], wo_ref[...], preferred_element_type=F32)


def _mixer_body(*refs, n_stage):
    n_in = 14
    (x_ref, nw_ref, win_ref, cw_ref, cos_ref, sin_ref, wm_ref, qd_ref, kd_ref, cd_ref,
     bias_ref, wg_ref, wb_ref, wo_ref) = refs[:n_in]
    stage_in = refs[n_in:n_in + n_stage]
    o_ref = refs[n_in + n_stage]
    stage_out = refs[n_in + n_stage + 1:n_in + 2 * n_stage + 1]
    (h_scr, z_scr, u_scr, yc_scr, yr_scr, ya_scr, rq_scr, rk_scr, rv_scr, rg_scr,
     aq_scr, k_scr, v_scr, st_scr, m_scr) = refs[n_in + 2 * n_stage + 1:]
    tm = x_ref.shape[0]
    j = pl.program_id(1)

    @pl.when(j == 0)
    def _():
        z_scr[0:8] = jnp.zeros((8, BRANCH_W), F32)
        st_scr[...] = jnp.zeros(st_scr.shape, F32)
        k_scr[0:tm] = jnp.zeros((tm, BRANCH_W), BF16)
        v_scr[0:tm] = jnp.zeros((tm, BRANCH_W), BF16)

    @pl.when(j > 0)
    def _():
        z_scr[0:8] = z_scr[tm:tm + 8]
        k_scr[0:tm] = k_scr[tm:2 * tm]
        v_scr[0:tm] = v_scr[tm:2 * tm]

    _cast_staged(stage_in, stage_out)
    h_scr[...] = _rms(x_ref[...], nw_ref[...]).astype(BF16)
    _project_inputs(h_scr, win_ref, cw_ref, cos_ref, sin_ref, z_scr, u_scr,
                    yc_scr, rq_scr, rk_scr, rv_scr, rg_scr, aq_scr, k_scr, v_scr)
    _retention_tile(rq_scr, rk_scr, rv_scr, rg_scr, wm_ref, qd_ref, kd_ref, cd_ref, st_scr, yr_scr)
    _attention_tile(j, aq_scr, k_scr, v_scr, bias_ref, ya_scr)
    _merge_tile(x_ref, h_scr, (yc_scr, yr_scr, ya_scr), wg_ref, wb_ref, wo_ref, m_scr, o_ref)


def _retention_tables(blk):
    log_gamma = jnp.log1p(-jnp.exp2(-5.0 - jnp.arange(H_RET, dtype=F32)))
    pos = jnp.arange(blk, dtype=F32)
    chunk = jnp.arange(blk) // CHUNK
    scale = DK_RET ** -0.5
    decay = jnp.exp(log_gamma[:, None, None] * jnp.abs(pos[:, None] - pos[None, :]))
    wm = jnp.where(chunk[None, :, None] >= chunk[None, None, :], decay, 0.0) * scale
    qd = jnp.exp(log_gamma[:, None] * (pos + 1.0)) * scale
    kd = jnp.exp(log_gamma[:, None] * (blk - 1.0 - pos))
    cd = jnp.exp(log_gamma * blk)
    bc = lambda a: jnp.broadcast_to(a[..., None], a.shape + (DK_RET,))
    return wm, bc(qd), bc(kd), bc(cd)[:, None, :]


def _attention_bias(rel_bias):
    h = rel_bias.shape[0]
    back = N_PREV_CHUNKS * CHUNK
    period = ATT_WIN + ATT_SUB
    far = rel_bias[:, -1:].astype(F32)
    row = jnp.concatenate([jnp.broadcast_to(far, (h, back - REL_CLIP)),
                           rel_bias[:, ::-1].astype(F32),
                           jnp.broadcast_to(far, (h, period - back - REL_CLIP - 1))], axis=1)
    flat = jnp.broadcast_to(row[:, None, :], (h, ATT_SUB, period)).reshape(h, ATT_SUB * period)
    skew = flat[:, :ATT_SUB * (period - 1)].reshape(h, ATT_SUB, period - 1)[:, :, :ATT_WIN]
    qc = (jnp.arange(ATT_SUB) // CHUNK)[:, None]
    kc = (jnp.arange(ATT_WIN) // CHUNK)[None, :]
    band = (kc >= qc) & (kc <= qc + N_PREV_CHUNKS)
    return jnp.where(band[None], skew, NEG_INF)


def _mixer(x2d, bsz, nw, w_in, conv_w, cos2, sin2, ret_tables, bias, w_gate, w_branch, w_out,
           stage=()):
    t, d = x2d.shape
    s = t // bsz
    tm = MIX_TILE
    nt = s // tm
    bw = BRANCH_W
    row = pl.BlockSpec((tm, d), lambda b, j: (b * nt + j, 0))
    pos = pl.BlockSpec((tm, DK_RET), lambda b, j: (j, 0))
    consts = [w_in, conv_w, None, None, *ret_tables, bias, w_gate, w_branch, w_out]
    in_specs = [row, _const_spec((1, d))] + [pos if c is None else _const_spec(c.shape) for c in consts]
    tile = lambda dtype: pltpu.VMEM((tm, bw), dtype)
    scratch = [pltpu.VMEM((tm, d), BF16),
               pltpu.VMEM((tm + 8, bw), F32),
               tile(F32),
               tile(BF16), tile(BF16), tile(BF16),
               tile(BF16), tile(BF16), tile(BF16), tile(BF16),
               tile(BF16),
               pltpu.VMEM((2 * tm, bw), BF16), pltpu.VMEM((2 * tm, bw), BF16),
               pltpu.VMEM((H_RET, DK_RET, DK_RET), F32),
               pltpu.VMEM((tm, d), BF16)]
    st_in, st_out, st_shapes = _stage_specs(stage, bsz * nt, lambda b, j: b * nt + j)
    out = pl.pallas_call(
        functools.partial(_mixer_body, n_stage=len(stage)),
        grid=(bsz, nt),
        in_specs=in_specs + st_in,
        out_specs=[row] + st_out,
        out_shape=[jax.ShapeDtypeStruct((t, d), F32)] + st_shapes,
        scratch_shapes=scratch,
        compiler_params=_params(2, MIXER_VMEM_BYTES),
        name="mixer",
    )(x2d, nw, w_in, conv_w, cos2, sin2, *ret_tables, bias, w_gate, w_branch, w_out,
      *[w for w, _ in stage])
    return out[0], out[1:]


def kernel(x, ffn1_norm, ffn1_w_gate, ffn1_w_up, ffn1_w_down, mix_norm, w_in, conv_w, rel_bias,
           w_branch, w_merge_gate, w_out, ffn2_norm, ffn2_w_gate, ffn2_w_up, ffn2_w_down, final_norm):
    bsz, s, d = x.shape
    depth = w_in.shape[0]
    assert (bsz * s) % FFN_TILE == 0 and s % MIX_TILE == 0
    assert MIX_TILE % RET_BLOCK == 0 and RET_BLOCK % CHUNK == 0 and MIX_TILE % ATT_SUB == 0

    inv_freq = ROPE_BASE ** (-jnp.linspace(0.0, 1.0, DK_RET // 2, dtype=F32))
    ang = jnp.arange(s, dtype=F32)[:, None] * inv_freq[None, :]
    cos, sin = jnp.cos(ang), jnp.sin(ang)
    cos2 = jnp.concatenate([cos, cos], axis=-1)
    sin2 = jnp.concatenate([-sin, sin], axis=-1)
    ret_tables = _retention_tables(RET_BLOCK)

    vec = lambda w: w.reshape(1, d).astype(F32)
    gate_rows = w_merge_gate.reshape(depth, -1, d)
    branch_rows = w_branch.reshape(depth, -1, d)
    mixer_weights = lambda l: [(w_in, l), (gate_rows, l), (branch_rows, l), (w_out, l)]
    ffn1_weights = lambda l: [(ffn1_w_gate, l), (ffn1_w_up, l), (ffn1_w_down, l)]
    ffn2_weights = lambda l: [(ffn2_w_gate, l), (ffn2_w_up, l), (ffn2_w_down, l)]

    xf = x.reshape(bsz * s, d)
    ffn_w = [w[0].astype(BF16) for w, _ in ffn1_weights(0)]
    for l in range(depth):
        last = l == depth - 1
        xf, staged = _ffn(xf, vec(ffn1_norm[l]), *ffn_w, stage=mixer_weights(l) + ffn2_weights(l))
        w_in_b, w_gate_b, w_branch_b, w_out_b = staged[:4]
        ffn_w = staged[4:]
        xf, _ = _mixer(xf, bsz, vec(mix_norm[l]), w_in_b, conv_w[l].astype(F32), cos2, sin2,
                       ret_tables, _attention_bias(rel_bias[l]), w_gate_b, w_branch_b, w_out_b)
        xf, ffn_w = _ffn(xf, vec(ffn2_norm[l]), *ffn_w, fw=vec(final_norm) if last else None,
                         stage=() if last else ffn1_weights(l + 1))
    return xf.reshape(bsz, s, d)
```

```python
import functools

import jax
import jax.numpy as jnp
from jax import lax
from jax.experimental import pallas as pl
from jax.experimental.pallas import tpu as pltpu

F32 = jnp.float32
BF16 = jnp.bfloat16

CHUNK = 64
BRANCH_W = 512
H_RET = 4
DK_RET = BRANCH_W // H_RET
H_ATT = 8
DH_ATT = BRANCH_W // H_ATT
N_PREV_CHUNKS = 8
REL_CLIP = 128
EPS = 1e-6
NEG_INF = -1e30
ROPE_BASE = 10000.0
LOG2E = 1.4426950408889634
ATT_Q_SCALE = DH_ATT ** -0.5 * LOG2E

MXU_COLS = 256
LANES = 128
BF16_SUBLANES = 16
MIB = 1024 * 1024
FFN_VMEM_BYTES = 50 * MIB
MIXER_VMEM_BYTES = 60 * MIB

FFN_TILE = 1024
FFN_SUB = 256
FF_CHUNK = 256
MIX_TILE = N_PREV_CHUNKS * CHUNK
RET_BLOCK = 256
ATT_SUB = 128
ATT_WIN = ATT_SUB + N_PREV_CHUNKS * CHUNK
HEAD_GROUP = MXU_COLS // DH_ATT

NT_DIMS = (((1,), (1,)), ((), ()))
TN_DIMS = (((0,), (0,)), ((), ()))


def _const_spec(shape):
    zeros = (0,) * len(shape)
    return pl.BlockSpec(shape, lambda *_: zeros, pipeline_mode=pl.Buffered(1))


def _params(n_axes, vmem_bytes):
    return pltpu.CompilerParams(dimension_semantics=("arbitrary",) * n_axes,
                                vmem_limit_bytes=vmem_bytes)


def _stage_specs(weights, n_steps, step_of):
    in_specs, out_specs, out_shapes = [], [], []
    for w, layer in weights:
        _, rows, cols = w.shape
        for n_col in (1, 2, 4, 8, 16, 32, 64, 128):
            n_row = n_steps // n_col
            if n_steps % n_col or rows % n_row or cols % n_col:
                continue
            br, bc = rows // n_row, cols // n_col
            if br % BF16_SUBLANES == 0 and bc % LANES == 0:
                break
        else:
            raise ValueError(f"no {n_steps}-block tiling for weight {w.shape}")
        in_specs.append(pl.BlockSpec(
            (None, br, bc),
            lambda *g, layer=layer, n_col=n_col: (layer, step_of(*g) // n_col, step_of(*g) % n_col)))
        out_specs.append(pl.BlockSpec(
            (br, bc), lambda *g, n_col=n_col: (step_of(*g) // n_col, step_of(*g) % n_col)))
        out_shapes.append(jax.ShapeDtypeStruct((rows, cols), BF16))
    return in_specs, out_specs, out_shapes


def _cast_staged(src_refs, dst_refs):
    for src, dst in zip(src_refs, dst_refs):
        dst[...] = src[...].astype(BF16)


def _rms(x, w):
    ms = jnp.mean(x * x, axis=-1, keepdims=True)
    return x * lax.rsqrt(ms + EPS) * w


def _silu(g):
    return g * jax.nn.sigmoid(g)


def _ffn_body(*refs, final, n_stage):
    n_in = 6 if final else 5
    x_ref, nw_ref, wg_ref, wu_ref, wd_ref = refs[:5]
    fw_ref = refs[5] if final else None
    stage_in = refs[n_in:n_in + n_stage]
    o_ref = refs[n_in + n_stage]
    stage_out = refs[n_in + n_stage + 1:n_in + 2 * n_stage + 1]
    h_scr, a_scr = refs[n_in + 2 * n_stage + 1:]
    _cast_staged(stage_in, stage_out)
    d_ff = wg_ref.shape[1]
    for r in range(x_ref.shape[0] // FFN_SUB):
        rows = slice(r * FFN_SUB, (r + 1) * FFN_SUB)
        h_scr[rows] = _rms(x_ref[rows], nw_ref[...]).astype(BF16)
        for c in range(d_ff // FF_CHUNK):
            sl = slice(c * FF_CHUNK, (c + 1) * FF_CHUNK)
            g = jnp.dot(h_scr[rows], wg_ref[:, sl], preferred_element_type=F32)
            u = jnp.dot(h_scr[rows], wu_ref[:, sl], preferred_element_type=F32)
            a_scr[rows, sl] = (_silu(g) * u).astype(BF16)
    y = x_ref[...] + 0.5 * jnp.dot(a_scr[...], wd_ref[...], preferred_element_type=F32)
    if final:
        y = _rms(y, fw_ref[...])
    o_ref[...] = y


def _ffn(x2d, nw, wg, wu, wd, fw=None, stage=()):
    t, d = x2d.shape
    d_ff = wg.shape[1]
    tm = FFN_TILE
    final = fw is not None
    row = pl.BlockSpec((tm, d), lambda i: (i, 0))
    in_specs = [row, _const_spec((1, d)), _const_spec((d, d_ff)), _const_spec((d, d_ff)),
                _const_spec((d_ff, d))]
    args = [x2d, nw, wg, wu, wd]
    if final:
        in_specs.append(_const_spec((1, d)))
        args.append(fw)
    st_in, st_out, st_shapes = _stage_specs(stage, t // tm, lambda i: i)
    out = pl.pallas_call(
        functools.partial(_ffn_body, final=final, n_stage=len(stage)),
        grid=(t // tm,),
        in_specs=in_specs + st_in,
        out_specs=[row] + st_out,
        out_shape=[jax.ShapeDtypeStruct((t, d), F32)] + st_shapes,
        scratch_shapes=[pltpu.VMEM((tm, d), BF16), pltpu.VMEM((tm, d_ff), BF16)],
        compiler_params=_params(1, FFN_VMEM_BYTES),
        name="ffn_final" if final else "ffn",
    )(*args, *[w for w, _ in stage])
    return out[0], out[1:]


def _project_inputs(h_scr, w_ref, cw_ref, cos_ref, sin_ref, z_scr, u_scr,
                    yc_scr, rq_scr, rk_scr, rv_scr, rg_scr, aq_scr, k_scr, v_scr):
    tm = h_scr.shape[0]
    bw = BRANCH_W

    def proj(g):
        return jnp.dot(h_scr[...], w_ref[:, g * bw:(g + 1) * bw], preferred_element_type=F32)

    u_scr[...] = proj(0)
    z_scr[8:tm + 8] = proj(2) * u_scr[...]
    conv = (cw_ref[0:1, :] * z_scr[6:tm + 6] + cw_ref[1:2, :] * z_scr[7:tm + 7]
            + cw_ref[2:3, :] * z_scr[8:tm + 8])
    yc_scr[...] = (proj(1) * conv).astype(BF16)

    cos2 = cos_ref[...]
    sin2 = sin_ref[...]
    for g, dst in ((3, rq_scr), (4, rk_scr)):
        r = proj(g)
        for hh in range(H_RET):
            sl = slice(hh * DK_RET, (hh + 1) * DK_RET)
            rh = r[:, sl]
            dst[:, sl] = (rh * cos2 + pltpu.roll(rh, DK_RET // 2, 1) * sin2).astype(BF16)
    rv_scr[...] = proj(5).astype(BF16)
    rg_scr[...] = proj(6).astype(BF16)
    aq_scr[...] = (proj(7) * ATT_Q_SCALE).astype(BF16)
    k_scr[tm:2 * tm] = proj(8).astype(BF16)
    v_scr[tm:2 * tm] = proj(9).astype(BF16)


def _retention_tile(q_scr, k_scr, v_scr, g_scr, wm_ref, qd_ref, kd_ref, cd_ref, st_scr, y_scr):
    blk = wm_ref.shape[1]
    heads = range(H_RET)
    sls = [slice(hh * DK_RET, (hh + 1) * DK_RET) for hh in heads]
    for b in range(q_scr.shape[0] // blk):
        rows = slice(b * blk, (b + 1) * blk)
        q = [q_scr[rows, sl] for sl in sls]
        k = [k_scr[rows, sl] for sl in sls]
        v = [v_scr[rows, sl] for sl in sls]
        st = [st_scr[hh] for hh in heads]
        inner = [(lax.dot_general(q[hh], k[hh], NT_DIMS, preferred_element_type=F32)
                  * wm_ref[hh]).astype(BF16) for hh in heads]
        cross = [jnp.dot(q[hh], st[hh].astype(BF16), preferred_element_type=F32) * qd_ref[hh]
                 for hh in heads]
        for hh in heads:
            kdec = (k[hh].astype(F32) * kd_ref[hh]).astype(BF16)
            st_scr[hh] = st[hh] * cd_ref[hh] + lax.dot_general(kdec, v[hh], TN_DIMS,
                                                               preferred_element_type=F32)
        for hh in heads:
            o = jnp.dot(inner[hh], v[hh], preferred_element_type=F32) + cross[hh]
            o = o * lax.rsqrt(jnp.mean(o * o, axis=-1, keepdims=True) + EPS)
            y_scr[rows, sls[hh]] = (_silu(g_scr[rows, sls[hh]].astype(F32)) * o).astype(BF16)


def _attention_tile(tile_idx, q_scr, k_scr, v_scr, bias_ref, y_scr):
    qb = q_scr.shape[0]
    lane_head = lax.broadcasted_iota(jnp.int32, (ATT_SUB, MXU_COLS), 1) // DH_ATT
    col = lax.broadcasted_iota(jnp.int32, (ATT_SUB, ATT_WIN), 1)
    for t in range(qb // ATT_SUB):
        r0 = t * ATT_SUB
        valid = col >= (1 - tile_idx) * qb - r0
        for g in range(H_ATT // HEAD_GROUP):
            cs = slice(g * MXU_COLS, (g + 1) * MXU_COLS)
            qg = q_scr[r0:r0 + ATT_SUB, cs].astype(F32)
            qs = jnp.concatenate(
                [jnp.where(lane_head == hh, qg, 0.0).astype(BF16) for hh in range(HEAD_GROUP)], axis=0)
            sc = lax.dot_general(qs, k_scr[r0:r0 + ATT_WIN, cs], NT_DIMS,
                                 preferred_element_type=F32)
            es, inv = [], []
            for hh in range(HEAD_GROUP):
                sh = sc[hh * ATT_SUB:(hh + 1) * ATT_SUB] + bias_ref[g * HEAD_GROUP + hh]
                sh = jnp.where(valid, sh, NEG_INF)
                e = jnp.exp2(sh - jnp.max(sh, axis=-1, keepdims=True))
                inv.append(1.0 / jnp.sum(e, axis=-1, keepdims=True))
                es.append(e.astype(BF16))
            pv = jnp.dot(jnp.concatenate(es, axis=0), v_scr[r0:r0 + ATT_WIN, cs],
                         preferred_element_type=F32)
            og = jnp.zeros((ATT_SUB, MXU_COLS), F32)
            for hh in range(HEAD_GROUP):
                og = jnp.where(lane_head == hh, pv[hh * ATT_SUB:(hh + 1) * ATT_SUB] * inv[hh], og)
            y_scr[r0:r0 + ATT_SUB, cs] = og.astype(BF16)


def _merge_tile(x_ref, h_scr, branches, wg_ref, wb_ref, wo_ref, m_scr, o_ref):
    d = x_ref.shape[1]
    bw = BRANCH_W
    for c in range(d // MXU_COLS):
        sl = slice(c * MXU_COLS, (c + 1) * MXU_COLS)
        acc = None
        for b, y_scr in enumerate(branches):
            gate = jax.nn.sigmoid(jnp.dot(h_scr[...], wg_ref[b * d:(b + 1) * d, sl],
                                          preferred_element_type=F32))
            term = gate * jnp.dot(y_scr[...], wb_ref[b * bw:(b + 1) * bw, sl],
                                  preferred_element_type=F32)
            acc = term if acc is None else acc + term
        m_scr[:, sl] = acc.astype(BF16)
    o_ref[...] = x_ref[...] + jnp.dot(m_scr[...], wo_ref[...], preferred_element_type=F32)


def _mixer_body(*refs, n_stage):
    n_in = 14
    (x_ref, nw_ref, win_ref, cw_ref, cos_ref, sin_ref, wm_ref, qd_ref, kd_ref, cd_ref,
     bias_ref, wg_ref, wb_ref, wo_ref) = refs[:n_in]
    stage_in = refs[n_in:n_in + n_stage]
    o_ref = refs[n_in + n_stage]
    stage_out = refs[n_in + n_stage + 1:n_in + 2 * n_stage + 1]
    (h_scr, z_scr, u_scr, yc_scr, yr_scr, ya_scr, rq_scr, rk_scr, rv_scr, rg_scr,
     aq_scr, k_scr, v_scr, st_scr, m_scr) = refs[n_in + 2 * n_stage + 1:]
    tm = x_ref.shape[0]
    j = pl.program_id(1)

    @pl.when(j == 0)
    def _():
        z_scr[0:8] = jnp.zeros((8, BRANCH_W), F32)
        st_scr[...] = jnp.zeros(st_scr.shape, F32)
        k_scr[0:tm] = jnp.zeros((tm, BRANCH_W), BF16)
        v_scr[0:tm] = jnp.zeros((tm, BRANCH_W), BF16)

    @pl.when(j > 0)
    def _():
        z_scr[0:8] = z_scr[tm:tm + 8]
        k_scr[0:tm] = k_scr[tm:2 * tm]
        v_scr[0:tm] = v_scr[tm:2 * tm]

    _cast_staged(stage_in, stage_out)
    h_scr[...] = _rms(x_ref[...], nw_ref[...]).astype(BF16)
    _project_inputs(h_scr, win_ref, cw_ref, cos_ref, sin_ref, z_scr, u_scr,
                    yc_scr, rq_scr, rk_scr, rv_scr, rg_scr, aq_scr, k_scr, v_scr)
    _retention_tile(rq_scr, rk_scr, rv_scr, rg_scr, wm_ref, qd_ref, kd_ref, cd_ref, st_scr, yr_scr)
    _attention_tile(j, aq_scr, k_scr, v_scr, bias_ref, ya_scr)
    _merge_tile(x_ref, h_scr, (yc_scr, yr_scr, ya_scr), wg_ref, wb_ref, wo_ref, m_scr, o_ref)


def _retention_tables(blk):
    log_gamma = jnp.log1p(-jnp.exp2(-5.0 - jnp.arange(H_RET, dtype=F32)))
    pos = jnp.arange(blk, dtype=F32)
    chunk = jnp.arange(blk) // CHUNK
    scale = DK_RET ** -0.5
    decay = jnp.exp(log_gamma[:, None, None] * jnp.abs(pos[:, None] - pos[None, :]))
    wm = jnp.where(chunk[None, :, None] >= chunk[None, None, :], decay, 0.0) * scale
    qd = jnp.exp(log_gamma[:, None] * (pos + 1.0)) * scale
    kd = jnp.exp(log_gamma[:, None] * (blk - 1.0 - pos))
    cd = jnp.exp(log_gamma * blk)
    bc = lambda a: jnp.broadcast_to(a[..., None], a.shape + (DK_RET,))
    return wm, bc(qd), bc(kd), bc(cd)[:, None, :]


def _rotary_tables(s):
    inv_freq = ROPE_BASE ** (-jnp.linspace(0.0, 1.0, DK_RET // 2, dtype=F32))
    ang = jnp.arange(s, dtype=F32)[:, None] * inv_freq[None, :]
    cos, sin = jnp.cos(ang), jnp.sin(ang)
    return jnp.concatenate([cos, cos], axis=-1), jnp.concatenate([-sin, sin], axis=-1)


def _attention_bias(rel_bias):
    h = rel_bias.shape[0]
    back = N_PREV_CHUNKS * CHUNK
    period = ATT_WIN + ATT_SUB
    far = rel_bias[:, -1:].astype(F32)
    row = jnp.concatenate([jnp.broadcast_to(far, (h, back - REL_CLIP)),
                           rel_bias[:, ::-1].astype(F32),
                           jnp.broadcast_to(far, (h, period - back - REL_CLIP - 1))], axis=1)
    flat = jnp.broadcast_to(row[:, None, :], (h, ATT_SUB, period)).reshape(h, ATT_SUB * period)
    skew = flat[:, :ATT_SUB * (period - 1)].reshape(h, ATT_SUB, period - 1)[:, :, :ATT_WIN]
    qc = (jnp.arange(ATT_SUB) // CHUNK)[:, None]
    kc = (jnp.arange(ATT_WIN) // CHUNK)[None, :]
    band = (kc >= qc) & (kc <= qc + N_PREV_CHUNKS)
    return jnp.where(band[None], skew * LOG2E, NEG_INF)


def _mixer(x2d, bsz, nw, w_in, conv_w, cos2, sin2, ret_tables, bias, w_gate, w_branch, w_out,
           stage=()):
    t, d = x2d.shape
    s = t // bsz
    tm = MIX_TILE
    nt = s // tm
    bw = BRANCH_W
    row = pl.BlockSpec((tm, d), lambda b, j: (b * nt + j, 0))
    pos = pl.BlockSpec((tm, DK_RET), lambda b, j: (j, 0))
    consts = [w_in, conv_w, None, None, *ret_tables, bias, w_gate, w_branch, w_out]
    in_specs = [row, _const_spec((1, d))] + [pos if c is None else _const_spec(c.shape) for c in consts]
    tile = lambda dtype: pltpu.VMEM((tm, bw), dtype)
    scratch = [pltpu.VMEM((tm, d), BF16),
               pltpu.VMEM((tm + 8, bw), F32),
               tile(F32),
               tile(BF16), tile(BF16), tile(BF16),
               tile(BF16), tile(BF16), tile(BF16), tile(BF16),
               tile(BF16),
               pltpu.VMEM((2 * tm, bw), BF16), pltpu.VMEM((2 * tm, bw), BF16),
               pltpu.VMEM((H_RET, DK_RET, DK_RET), F32),
               pltpu.VMEM((tm, d), BF16)]
    st_in, st_out, st_shapes = _stage_specs(stage, bsz * nt, lambda b, j: b * nt + j)
    out = pl.pallas_call(
        functools.partial(_mixer_body, n_stage=len(stage)),
        grid=(bsz, nt),
        in_specs=in_specs + st_in,
        out_specs=[row] + st_out,
        out_shape=[jax.ShapeDtypeStruct((t, d), F32)] + st_shapes,
        scratch_shapes=scratch,
        compiler_params=_params(2, MIXER_VMEM_BYTES),
        name="mixer",
    )(x2d, nw, w_in, conv_w, cos2, sin2, *ret_tables, bias, w_gate, w_branch, w_out,
      *[w for w, _ in stage])
    return out[0], out[1:]


def kernel(x, ffn1_norm, ffn1_w_gate, ffn1_w_up, ffn1_w_down, mix_norm, w_in, conv_w, rel_bias,
           w_branch, w_merge_gate, w_out, ffn2_norm, ffn2_w_gate, ffn2_w_up, ffn2_w_down, final_norm):
    bsz, s, d = x.shape
    depth = w_in.shape[0]
    assert (bsz * s) % FFN_TILE == 0 and s % MIX_TILE == 0
    assert MIX_TILE % RET_BLOCK == 0 and RET_BLOCK % CHUNK == 0 and MIX_TILE % ATT_SUB == 0

    cos2, sin2 = _rotary_tables(s)
    ret_tables = _retention_tables(RET_BLOCK)

    vec = lambda w: w.reshape(1, d).astype(F32)
    gate_rows = w_merge_gate.reshape(depth, -1, d)
    branch_rows = w_branch.reshape(depth, -1, d)
    mixer_weights = lambda l: [(w_in, l), (gate_rows, l), (branch_rows, l), (w_out, l)]
    ffn1_weights = lambda l: [(ffn1_w_gate, l), (ffn1_w_up, l), (ffn1_w_down, l)]
    ffn2_weights = lambda l: [(ffn2_w_gate, l), (ffn2_w_up, l), (ffn2_w_down, l)]

    xf = x.reshape(bsz * s, d)
    ffn_w = [w[0].astype(BF16) for w, _ in ffn1_weights(0)]
    for l in range(depth):
        last = l == depth - 1
        xf, staged = _ffn(xf, vec(ffn1_norm[l]), *ffn_w, stage=mixer_weights(l) + ffn2_weights(l))
        w_in_b, w_gate_b, w_branch_b, w_out_b = staged[:4]
        ffn_w = staged[4:]
        xf, _ = _mixer(xf, bsz, vec(mix_norm[l]), w_in_b, conv_w[l].astype(F32), cos2, sin2,
                       ret_tables, _attention_bias(rel_bias[l]), w_gate_b, w_branch_b, w_out_b)
        xf, ffn_w = _ffn(xf, vec(ffn2_norm[l]), *ffn_w, fw=vec(final_norm) if last else None,
                         stage=() if last else ffn1_weights(l + 1))
    return xf.reshape(bsz, s, d)
```

```python
import functools

import jax
import jax.numpy as jnp
import numpy as np
from jax import lax
from jax.experimental import pallas as pl
from jax.experimental.pallas import tpu as pltpu

F32 = jnp.float32
BF16 = jnp.bfloat16

CHUNK = 64
BRANCH_W = 512
H_RET = 4
DK_RET = BRANCH_W // H_RET
H_ATT = 8
DH_ATT = BRANCH_W // H_ATT
N_PREV_CHUNKS = 8
REL_CLIP = 128
EPS = 1e-6
NEG_INF = -1e30
ROPE_BASE = 10000.0
LOG2E = 1.4426950408889634
ATT_Q_SCALE = DH_ATT ** -0.5 * LOG2E

MXU_COLS = 256
LANES = 128
BF16_SUBLANES = 16
MIB = 1024 * 1024
FFN_VMEM_BYTES = 50 * MIB
MIXER_VMEM_BYTES = 60 * MIB

FFN_TILE = 1024
FFN_SUB = 256
FF_CHUNK = 256
MIX_TILE = N_PREV_CHUNKS * CHUNK
MIX_SUB = 256
RET_BLOCK = 256
ATT_SUB = 128
ATT_WIN = ATT_SUB + N_PREV_CHUNKS * CHUNK
HEAD_GROUP = MXU_COLS // DH_ATT

NT_DIMS = (((1,), (1,)), ((), ()))
TN_DIMS = (((0,), (0,)), ((), ()))


def _const_spec(shape):
    zeros = (0,) * len(shape)
    return pl.BlockSpec(shape, lambda *_: zeros, pipeline_mode=pl.Buffered(1))


def _params(n_axes, vmem_bytes):
    return pltpu.CompilerParams(dimension_semantics=("arbitrary",) * n_axes,
                                vmem_limit_bytes=vmem_bytes)


def _stage_specs(weights, n_steps, step_of):
    in_specs, out_specs, out_shapes = [], [], []
    for w, layer in weights:
        _, rows, cols = w.shape
        for n_col in (1, 2, 4, 8, 16, 32, 64, 128):
            n_row = n_steps // n_col
            if n_steps % n_col or rows % n_row or cols % n_col:
                continue
            br, bc = rows // n_row, cols // n_col
            if br % BF16_SUBLANES == 0 and bc % LANES == 0:
                break
        else:
            raise ValueError(f"no {n_steps}-block tiling for weight {w.shape}")
        in_specs.append(pl.BlockSpec(
            (None, br, bc),
            lambda *g, layer=layer, n_col=n_col: (layer, step_of(*g) // n_col, step_of(*g) % n_col)))
        out_specs.append(pl.BlockSpec(
            (br, bc), lambda *g, n_col=n_col: (step_of(*g) // n_col, step_of(*g) % n_col)))
        out_shapes.append(jax.ShapeDtypeStruct((rows, cols), BF16))
    return in_specs, out_specs, out_shapes


def _cast_staged(src_refs, dst_refs):
    for src, dst in zip(src_refs, dst_refs):
        dst[...] = src[...].astype(BF16)


def _rms(x, w):
    ms = jnp.mean(x * x, axis=-1, keepdims=True)
    return x * lax.rsqrt(ms + EPS) * w


def _silu(g):
    return g * jax.nn.sigmoid(g)


def _ffn_body(*refs, final, n_stage):
    n_in = 6 if final else 5
    x_ref, nw_ref, wg_ref, wu_ref, wd_ref = refs[:5]
    fw_ref = refs[5] if final else None
    stage_in = refs[n_in:n_in + n_stage]
    o_ref = refs[n_in + n_stage]
    stage_out = refs[n_in + n_stage + 1:n_in + 2 * n_stage + 1]
    h_scr, a_scr = refs[n_in + 2 * n_stage + 1:]
    _cast_staged(stage_in, stage_out)
    d_ff = wg_ref.shape[1]
    for r in range(x_ref.shape[0] // FFN_SUB):
        rows = slice(r * FFN_SUB, (r + 1) * FFN_SUB)
        h_scr[rows] = _rms(x_ref[rows], nw_ref[...]).astype(BF16)
        for c in range(d_ff // FF_CHUNK):
            sl = slice(c * FF_CHUNK, (c + 1) * FF_CHUNK)
            g = jnp.dot(h_scr[rows], wg_ref[:, sl], preferred_element_type=F32)
            u = jnp.dot(h_scr[rows], wu_ref[:, sl], preferred_element_type=F32)
            a_scr[rows, sl] = (_silu(g) * u).astype(BF16)
    y = x_ref[...] + 0.5 * jnp.dot(a_scr[...], wd_ref[...], preferred_element_type=F32)
    if final:
        y = _rms(y, fw_ref[...])
    o_ref[...] = y


def _ffn(x2d, nw, wg, wu, wd, fw=None, stage=()):
    t, d = x2d.shape
    d_ff = wg.shape[1]
    tm = FFN_TILE
    final = fw is not None
    row = pl.BlockSpec((tm, d), lambda i: (i, 0))
    in_specs = [row, _const_spec((1, d)), _const_spec((d, d_ff)), _const_spec((d, d_ff)),
                _const_spec((d_ff, d))]
    args = [x2d, nw, wg, wu, wd]
    if final:
        in_specs.append(_const_spec((1, d)))
        args.append(fw)
    st_in, st_out, st_shapes = _stage_specs(stage, t // tm, lambda i: i)
    out = pl.pallas_call(
        functools.partial(_ffn_body, final=final, n_stage=len(stage)),
        grid=(t // tm,),
        in_specs=in_specs + st_in,
        out_specs=[row] + st_out,
        out_shape=[jax.ShapeDtypeStruct((t, d), F32)] + st_shapes,
        scratch_shapes=[pltpu.VMEM((tm, d), BF16), pltpu.VMEM((tm, d_ff), BF16)],
        compiler_params=_params(1, FFN_VMEM_BYTES),
        name="ffn_final" if final else "ffn",
    )(*args, *[w for w, _ in stage])
    return out[0], out[1:]


def _project_inputs(par, x_ref, nw_ref, h_scr, w_ref, cw_ref, cos_ref, sin_ref, z_scr, u_scr,
                    yc_scr, rq_scr, rk_scr, rv_scr, rg_scr, aq_scr, k_scr, v_scr):
    tm = h_scr.shape[0]
    bw = BRANCH_W
    for r in range(tm // MIX_SUB):
        lo, hi = r * MIX_SUB, (r + 1) * MIX_SUB
        rows = slice(lo, hi)
        h_scr[rows] = _rms(x_ref[rows], nw_ref[...]).astype(BF16)

        def proj(g):
            return jnp.dot(h_scr[rows], w_ref[:, g * bw:(g + 1) * bw], preferred_element_type=F32)

        u_scr[rows] = proj(0)
        z_scr[lo + 8:hi + 8] = proj(2) * u_scr[rows]
        conv = (cw_ref[0:1, :] * z_scr[lo + 6:hi + 6] + cw_ref[1:2, :] * z_scr[lo + 7:hi + 7]
                + cw_ref[2:3, :] * z_scr[lo + 8:hi + 8])
        yc_scr[rows] = (proj(1) * conv).astype(BF16)

        cos2 = cos_ref[rows]
        sin2 = sin_ref[rows]
        for g, dst in ((3, rq_scr), (4, rk_scr)):
            p = proj(g)
            for hh in range(H_RET):
                sl = slice(hh * DK_RET, (hh + 1) * DK_RET)
                ph = p[:, sl]
                dst[rows, sl] = (ph * cos2 + pltpu.roll(ph, DK_RET // 2, 1) * sin2).astype(BF16)
        rv_scr[rows] = proj(5).astype(BF16)
        rg_scr[rows] = proj(6).astype(BF16)
        aq_scr[rows] = (proj(7) * ATT_Q_SCALE).astype(BF16)
        kb = proj(8).astype(BF16)
        k_scr[par, tm + lo:tm + hi] = kb
        k_scr[1 - par, lo:hi] = kb
        vb = proj(9).astype(BF16)
        v_scr[par, tm + lo:tm + hi] = vb
        v_scr[1 - par, lo:hi] = vb


def _retention_tile(q_scr, k_scr, v_scr, g_scr, wm_ref, qd_ref, kd_ref, cd_ref, st_scr, y_scr):
    blk = wm_ref.shape[1]
    heads = range(H_RET)
    sls = [slice(hh * DK_RET, (hh + 1) * DK_RET) for hh in heads]
    for b in range(q_scr.shape[0] // blk):
        rows = slice(b * blk, (b + 1) * blk)
        q = [q_scr[rows, sl] for sl in sls]
        k = [k_scr[rows, sl] for sl in sls]
        v = [v_scr[rows, sl] for sl in sls]
        st = [st_scr[hh] for hh in heads]
        inner = [(lax.dot_general(q[hh], k[hh], NT_DIMS, preferred_element_type=F32)
                  * wm_ref[hh]).astype(BF16) for hh in heads]
        cross = [jnp.dot(q[hh], st[hh].astype(BF16), preferred_element_type=F32) * qd_ref[hh]
                 for hh in heads]
        for hh in heads:
            kdec = (k[hh].astype(F32) * kd_ref[hh]).astype(BF16)
            st_scr[hh] = st[hh] * cd_ref[hh] + lax.dot_general(kdec, v[hh], TN_DIMS,
                                                               preferred_element_type=F32)
        for hh in heads:
            o = jnp.dot(inner[hh], v[hh], preferred_element_type=F32) + cross[hh]
            o = o * lax.rsqrt(jnp.mean(o * o, axis=-1, keepdims=True) + EPS)
            y_scr[rows, sls[hh]] = (_silu(g_scr[rows, sls[hh]].astype(F32)) * o).astype(BF16)


def _attention_tile(tile_idx, par, q_scr, k_scr, v_scr, bias_ref, y_scr):
    qb = q_scr.shape[0]
    lane_head = lax.broadcasted_iota(jnp.int32, (ATT_SUB, MXU_COLS), 1) // DH_ATT
    col = lax.broadcasted_iota(jnp.int32, (ATT_SUB, ATT_WIN), 1)
    for t in range(qb // ATT_SUB):
        r0 = t * ATT_SUB
        valid = col >= (1 - tile_idx) * qb - r0
        for g in range(H_ATT // HEAD_GROUP):
            cs = slice(g * MXU_COLS, (g + 1) * MXU_COLS)
            qg = q_scr[r0:r0 + ATT_SUB, cs].astype(F32)
            qs = jnp.concatenate(
                [jnp.where(lane_head == hh, qg, 0.0).astype(BF16) for hh in range(HEAD_GROUP)], axis=0)
            sc = lax.dot_general(qs, k_scr[par, r0:r0 + ATT_WIN, cs], NT_DIMS,
                                 preferred_element_type=F32)
            es, inv = [], []
            for hh in range(HEAD_GROUP):
                sh = sc[hh * ATT_SUB:(hh + 1) * ATT_SUB] + bias_ref[g * HEAD_GROUP + hh]
                sh = jnp.where(valid, sh, NEG_INF)
                e = jnp.exp2(sh - jnp.max(sh, axis=-1, keepdims=True))
                inv.append(1.0 / jnp.sum(e, axis=-1, keepdims=True))
                es.append(e.astype(BF16))
            pv = jnp.dot(jnp.concatenate(es, axis=0), v_scr[par, r0:r0 + ATT_WIN, cs],
                         preferred_element_type=F32)
            og = jnp.zeros((ATT_SUB, MXU_COLS), F32)
            for hh in range(HEAD_GROUP):
                og = jnp.where(lane_head == hh, pv[hh * ATT_SUB:(hh + 1) * ATT_SUB] * inv[hh], og)
            y_scr[r0:r0 + ATT_SUB, cs] = og.astype(BF16)


def _merge_tile(x_ref, h_scr, branches, wg_ref, wb_ref, wo_ref, m_scr, o_ref):
    d = x_ref.shape[1]
    bw = BRANCH_W
    for c in range(d // MXU_COLS):
        sl = slice(c * MXU_COLS, (c + 1) * MXU_COLS)
        acc = None
        for b, y_scr in enumerate(branches):
            gate = jax.nn.sigmoid(jnp.dot(h_scr[...], wg_ref[b * d:(b + 1) * d, sl],
                                          preferred_element_type=F32))
            term = gate * jnp.dot(y_scr[...], wb_ref[b * bw:(b + 1) * bw, sl],
                                  preferred_element_type=F32)
            acc = term if acc is None else acc + term
        m_scr[:, sl] = acc.astype(BF16)
    o_ref[...] = x_ref[...] + jnp.dot(m_scr[...], wo_ref[...], preferred_element_type=F32)


def _mixer_body(*refs, n_stage):
    n_in = 14
    (x_ref, nw_ref, win_ref, cw_ref, cos_ref, sin_ref, wm_ref, qd_ref, kd_ref, cd_ref,
     bias_ref, wg_ref, wb_ref, wo_ref) = refs[:n_in]
    stage_in = refs[n_in:n_in + n_stage]
    o_ref = refs[n_in + n_stage]
    stage_out = refs[n_in + n_stage + 1:n_in + 2 * n_stage + 1]
    (h_scr, z_scr, u_scr, yc_scr, yr_scr, ya_scr, rq_scr, rk_scr, rv_scr, rg_scr,
     aq_scr, k_scr, v_scr, st_scr, m_scr) = refs[n_in + 2 * n_stage + 1:]
    tm = x_ref.shape[0]
    j = pl.program_id(1)
    par = j % 2

    @pl.when(j == 0)
    def _():
        z_scr[0:8] = jnp.zeros((8, BRANCH_W), F32)
        st_scr[...] = jnp.zeros(st_scr.shape, F32)
        k_scr[0, 0:tm] = jnp.zeros((tm, BRANCH_W), BF16)
        v_scr[0, 0:tm] = jnp.zeros((tm, BRANCH_W), BF16)

    @pl.when(j > 0)
    def _():
        z_scr[0:8] = z_scr[tm:tm + 8]

    _cast_staged(stage_in, stage_out)
    _project_inputs(par, x_ref, nw_ref, h_scr, win_ref, cw_ref, cos_ref, sin_ref, z_scr, u_scr,
                    yc_scr, rq_scr, rk_scr, rv_scr, rg_scr, aq_scr, k_scr, v_scr)
    _retention_tile(rq_scr, rk_scr, rv_scr, rg_scr, wm_ref, qd_ref, kd_ref, cd_ref, st_scr, yr_scr)
    _attention_tile(j, par, aq_scr, k_scr, v_scr, bias_ref, ya_scr)
    _merge_tile(x_ref, h_scr, (yc_scr, yr_scr, ya_scr), wg_ref, wb_ref, wo_ref, m_scr, o_ref)


def _retention_tables(blk):
    f32 = np.float32
    log_gamma = np.log1p(-np.exp2(f32(-5.0) - np.arange(H_RET, dtype=f32)))
    pos = np.arange(blk, dtype=f32)
    chunk = np.arange(blk) // CHUNK
    scale = f32(DK_RET ** -0.5)
    decay = np.exp(log_gamma[:, None, None] * np.abs(pos[:, None] - pos[None, :]))
    wm = np.where(chunk[None, :, None] >= chunk[None, None, :], decay, f32(0.0)) * scale
    qd = np.exp(log_gamma[:, None] * (pos + f32(1.0))) * scale
    kd = np.exp(log_gamma[:, None] * (f32(blk - 1.0) - pos))
    cd = np.exp(log_gamma * f32(blk))
    bc = lambda a: np.ascontiguousarray(np.broadcast_to(a[..., None], a.shape + (DK_RET,)), dtype=f32)
    return tuple(jnp.asarray(t, dtype=F32) for t in (wm, bc(qd), bc(kd), bc(cd)[:, None, :]))


def _rotary_tables(s):
    f32 = np.float32
    inv_freq = f32(ROPE_BASE) ** (-np.linspace(0.0, 1.0, DK_RET // 2, dtype=f32))
    ang = np.arange(s, dtype=f32)[:, None] * inv_freq[None, :]
    cos, sin = np.cos(ang), np.sin(ang)
    cos2 = np.concatenate([cos, cos], axis=-1)
    sin2 = np.concatenate([-sin, sin], axis=-1)
    return jnp.asarray(cos2, dtype=F32), jnp.asarray(sin2, dtype=F32)


def _attention_bias(rel_bias):
    h = rel_bias.shape[0]
    back = N_PREV_CHUNKS * CHUNK
    period = ATT_WIN + ATT_SUB
    far = rel_bias[:, -1:].astype(F32)
    row = jnp.concatenate([jnp.broadcast_to(far, (h, back - REL_CLIP)),
                           rel_bias[:, ::-1].astype(F32),
                           jnp.broadcast_to(far, (h, period - back - REL_CLIP - 1))], axis=1)
    flat = jnp.broadcast_to(row[:, None, :], (h, ATT_SUB, period)).reshape(h, ATT_SUB * period)
    skew = flat[:, :ATT_SUB * (period - 1)].reshape(h, ATT_SUB, period - 1)[:, :, :ATT_WIN]
    qc = (jnp.arange(ATT_SUB) // CHUNK)[:, None]
    kc = (jnp.arange(ATT_WIN) // CHUNK)[None, :]
    band = (kc >= qc) & (kc <= qc + N_PREV_CHUNKS)
    return jnp.where(band[None], skew * LOG2E, NEG_INF)


def _mixer(x2d, bsz, nw, w_in, conv_w, cos2, sin2, ret_tables, bias, w_gate, w_branch, w_out,
           stage=()):
    t, d = x2d.shape
    s = t // bsz
    tm = MIX_TILE
    nt = s // tm
    bw = BRANCH_W
    row = pl.BlockSpec((tm, d), lambda b, j: (b * nt + j, 0))
    pos = pl.BlockSpec((tm, DK_RET), lambda b, j: (j, 0))
    consts = [w_in, conv_w, None, None, *ret_tables, bias, w_gate, w_branch, w_out]
    in_specs = [row, _const_spec((1, d))] + [pos if c is None else _const_spec(c.shape) for c in consts]
    tile = lambda dtype: pltpu.VMEM((tm, bw), dtype)
    scratch = [pltpu.VMEM((tm, d), BF16),
               pltpu.VMEM((tm + 8, bw), F32),
               tile(F32),
               tile(BF16), tile(BF16), tile(BF16),
               tile(BF16), tile(BF16), tile(BF16), tile(BF16),
               tile(BF16),
               pltpu.VMEM((2, 2 * tm, bw), BF16), pltpu.VMEM((2, 2 * tm, bw), BF16),
               pltpu.VMEM((H_RET, DK_RET, DK_RET), F32),
               pltpu.VMEM((tm, d), BF16)]
    st_in, st_out, st_shapes = _stage_specs(stage, bsz * nt, lambda b, j: b * nt + j)
    out = pl.pallas_call(
        functools.partial(_mixer_body, n_stage=len(stage)),
        grid=(bsz, nt),
        in_specs=in_specs + st_in,
        out_specs=[row] + st_out,
        out_shape=[jax.ShapeDtypeStruct((t, d), F32)] + st_shapes,
        scratch_shapes=scratch,
        compiler_params=_params(2, MIXER_VMEM_BYTES),
        name="mixer",
    )(x2d, nw, w_in, conv_w, cos2, sin2, *ret_tables, bias, w_gate, w_branch, w_out,
      *[w for w, _ in stage])
    return out[0], out[1:]


def kernel(x, ffn1_norm, ffn1_w_gate, ffn1_w_up, ffn1_w_down, mix_norm, w_in, conv_w, rel_bias,
           w_branch, w_merge_gate, w_out, ffn2_norm, ffn2_w_gate, ffn2_w_up, ffn2_w_down, final_norm):
    bsz, s, d = x.shape
    depth = w_in.shape[0]
    assert (bsz * s) % FFN_TILE == 0 and s % MIX_TILE == 0
    assert MIX_TILE % RET_BLOCK == 0 and RET_BLOCK % CHUNK == 0 and MIX_TILE % ATT_SUB == 0
    assert MIX_TILE % MIX_SUB == 0 and FFN_TILE % FFN_SUB == 0

    cos2, sin2 = _rotary_tables(s)
    ret_tables = _retention_tables(RET_BLOCK)

    vec = lambda w: w.reshape(1, d).astype(F32)
    gate_rows = w_merge_gate.reshape(depth, -1, d)
    branch_rows = w_branch.reshape(depth, -1, d)
    mixer_weights = lambda l: [(w_in, l), (gate_rows, l), (branch_rows, l), (w_out, l)]
    ffn1_weights = lambda l: [(ffn1_w_gate, l), (ffn1_w_up, l), (ffn1_w_down, l)]
    ffn2_weights = lambda l: [(ffn2_w_gate, l), (ffn2_w_up, l), (ffn2_w_down, l)]

    xf = x.reshape(bsz * s, d)
    ffn_w = [w[0].astype(BF16) for w, _ in ffn1_weights(0)]
    for l in range(depth):
        last = l == depth - 1
        xf, staged = _ffn(xf, vec(ffn1_norm[l]), *ffn_w, stage=mixer_weights(l) + ffn2_weights(l))
        w_in_b, w_gate_b, w_branch_b, w_out_b = staged[:4]
        ffn_w = staged[4:]
        xf, _ = _mixer(xf, bsz, vec(mix_norm[l]), w_in_b, conv_w[l].astype(F32), cos2, sin2,
                       ret_tables, _attention_bias(rel_bias[l]), w_gate_b, w_branch_b, w_out_b)
        xf, ffn_w = _ffn(xf, vec(ffn2_norm[l]), *ffn_w, fw=vec(final_norm) if last else None,
                         stage=() if last else ffn1_weights(l + 1))
    return xf.reshape(bsz, s, d)
```

```python
import functools

import jax
import jax.numpy as jnp
import numpy as np
from jax import lax
from jax.experimental import pallas as pl
from jax.experimental.pallas import tpu as pltpu

F32 = jnp.float32
BF16 = jnp.bfloat16

CHUNK = 64
BRANCH_W = 512
H_RET = 4
DK_RET = BRANCH_W // H_RET
H_ATT = 8
DH_ATT = BRANCH_W // H_ATT
N_PREV_CHUNKS = 8
REL_CLIP = 128
EPS = 1e-6
NEG_INF = -1e30
ROPE_BASE = 10000.0
LOG2E = 1.4426950408889634
ATT_Q_SCALE = DH_ATT ** -0.5 * LOG2E

MXU_COLS = 256
LANES = 128
BF16_SUBLANES = 16
MIB = 1024 * 1024
FFN_VMEM_BYTES = 50 * MIB
CAST_VMEM_BYTES = 16 * MIB
MIXER_VMEM_BYTES = 60 * MIB

CAST_STEPS = 16
FFN_TILE = 1024
FFN_SUB = 256
FF_CHUNK = 256
MIX_TILE = N_PREV_CHUNKS * CHUNK
RET_BLOCK = 256
ATT_SUB = 128
ATT_WIN = ATT_SUB + N_PREV_CHUNKS * CHUNK
HEAD_GROUP = MXU_COLS // DH_ATT

NT_DIMS = (((1,), (1,)), ((), ()))
TN_DIMS = (((0,), (0,)), ((), ()))


def _const_spec(shape):
    zeros = (0,) * len(shape)
    return pl.BlockSpec(shape, lambda *_: zeros, pipeline_mode=pl.Buffered(1))


def _params(n_axes, vmem_bytes):
    return pltpu.CompilerParams(dimension_semantics=("arbitrary",) * n_axes,
                                vmem_limit_bytes=vmem_bytes)


def _stage_specs(weights, n_steps, step_of):
    in_specs, out_specs, out_shapes = [], [], []
    for w, layer in weights:
        _, rows, cols = w.shape
        for n_col in (1, 2, 4, 8, 16, 32, 64, 128):
            n_row = n_steps // n_col
            if n_steps % n_col or rows % n_row or cols % n_col:
                continue
            br, bc = rows // n_row, cols // n_col
            if br % BF16_SUBLANES == 0 and bc % LANES == 0:
                break
        else:
            raise ValueError(f"no {n_steps}-block tiling for weight {w.shape}")
        in_specs.append(pl.BlockSpec(
            (None, br, bc),
            lambda *g, layer=layer, n_col=n_col: (layer, step_of(*g) // n_col, step_of(*g) % n_col)))
        out_specs.append(pl.BlockSpec(
            (br, bc), lambda *g, n_col=n_col: (step_of(*g) // n_col, step_of(*g) % n_col)))
        out_shapes.append(jax.ShapeDtypeStruct((rows, cols), BF16))
    return in_specs, out_specs, out_shapes


def _cast_staged(src_refs, dst_refs):
    for src, dst in zip(src_refs, dst_refs):
        dst[...] = src[...].astype(BF16)


def _rms(x, w):
    ms = jnp.mean(x * x, axis=-1, keepdims=True)
    return x * lax.rsqrt(ms + EPS) * w


def _silu(g):
    return g * jax.nn.sigmoid(g)


def _cast_body(*refs):
    n = len(refs) // 2
    _cast_staged(refs[:n], refs[n:])


def _cast_weights(weights):
    st_in, st_out, st_shapes = _stage_specs(weights, CAST_STEPS, lambda i: i)
    return pl.pallas_call(
        _cast_body,
        grid=(CAST_STEPS,),
        in_specs=st_in,
        out_specs=st_out,
        out_shape=st_shapes,
        compiler_params=_params(1, CAST_VMEM_BYTES),
        name="cast",
    )(*[w for w, _ in weights])


def _ffn_body(*refs, final, n_stage):
    n_in = 6 if final else 5
    x_ref, nw_ref, wg_ref, wu_ref, wd_ref = refs[:5]
    fw_ref = refs[5] if final else None
    stage_in = refs[n_in:n_in + n_stage]
    o_ref = refs[n_in + n_stage]
    stage_out = refs[n_in + n_stage + 1:n_in + 2 * n_stage + 1]
    h_scr, a_scr = refs[n_in + 2 * n_stage + 1:]
    _cast_staged(stage_in, stage_out)
    d_ff = wg_ref.shape[1]
    for r in range(x_ref.shape[0] // FFN_SUB):
        rows = slice(r * FFN_SUB, (r + 1) * FFN_SUB)
        h_scr[rows] = _rms(x_ref[rows], nw_ref[...]).astype(BF16)
        for c in range(d_ff // FF_CHUNK):
            sl = slice(c * FF_CHUNK, (c + 1) * FF_CHUNK)
            g = jnp.dot(h_scr[rows], wg_ref[:, sl], preferred_element_type=F32)
            u = jnp.dot(h_scr[rows], wu_ref[:, sl], preferred_element_type=F32)
            a_scr[rows, sl] = (_silu(g) * u).astype(BF16)
    y = x_ref[...] + 0.5 * jnp.dot(a_scr[...], wd_ref[...], preferred_element_type=F32)
    if final:
        y = _rms(y, fw_ref[...])
    o_ref[...] = y


def _ffn(x2d, nw, wg, wu, wd, fw=None, stage=()):
    t, d = x2d.shape
    d_ff = wg.shape[1]
    tm = FFN_TILE
    final = fw is not None
    row = pl.BlockSpec((tm, d), lambda i: (i, 0))
    in_specs = [row, _const_spec((1, d)), _const_spec((d, d_ff)), _const_spec((d, d_ff)),
                _const_spec((d_ff, d))]
    args = [x2d, nw, wg, wu, wd]
    if final:
        in_specs.append(_const_spec((1, d)))
        args.append(fw)
    st_in, st_out, st_shapes = _stage_specs(stage, t // tm, lambda i: i)
    out = pl.pallas_call(
        functools.partial(_ffn_body, final=final, n_stage=len(stage)),
        grid=(t // tm,),
        in_specs=in_specs + st_in,
        out_specs=[row] + st_out,
        out_shape=[jax.ShapeDtypeStruct((t, d), F32)] + st_shapes,
        scratch_shapes=[pltpu.VMEM((tm, d), BF16), pltpu.VMEM((tm, d_ff), BF16)],
        compiler_params=_params(1, FFN_VMEM_BYTES),
        name="ffn_final" if final else "ffn",
    )(*args, *[w for w, _ in stage])
    return out[0], out[1:]


def _project_inputs(h_scr, w_ref, cw_ref, cos_ref, sin_ref, z_scr, u_scr,
                    yc_scr, rq_scr, rk_scr, rv_scr, rg_scr, aq_scr, k_scr, v_scr):
    tm = h_scr.shape[0]
    bw = BRANCH_W

    def proj(g):
        return jnp.dot(h_scr[...], w_ref[:, g * bw:(g + 1) * bw], preferred_element_type=F32)

    u_scr[...] = proj(0)
    z_scr[8:tm + 8] = proj(2) * u_scr[...]
    conv = (cw_ref[0:1, :] * z_scr[6:tm + 6] + cw_ref[1:2, :] * z_scr[7:tm + 7]
            + cw_ref[2:3, :] * z_scr[8:tm + 8])
    yc_scr[...] = (proj(1) * conv).astype(BF16)

    cos2 = cos_ref[...]
    sin2 = sin_ref[...]
    for g, dst in ((3, rq_scr), (4, rk_scr)):
        r = proj(g)
        for hh in range(H_RET):
            sl = slice(hh * DK_RET, (hh + 1) * DK_RET)
            rh = r[:, sl]
            dst[:, sl] = (rh * cos2 + pltpu.roll(rh, DK_RET // 2, 1) * sin2).astype(BF16)
    rv_scr[...] = proj(5).astype(BF16)
    rg_scr[...] = proj(6).astype(BF16)
    aq_scr[...] = (proj(7) * ATT_Q_SCALE).astype(BF16)
    k_scr[tm:2 * tm] = proj(8).astype(BF16)
    v_scr[tm:2 * tm] = proj(9).astype(BF16)


def _retention_tile(q_scr, k_scr, v_scr, g_scr, wm_ref, qd_ref, kd_ref, cd_ref, st_scr, y_scr):
    blk = wm_ref.shape[1]
    heads = range(H_RET)
    sls = [slice(hh * DK_RET, (hh + 1) * DK_RET) for hh in heads]
    for b in range(q_scr.shape[0] // blk):
        rows = slice(b * blk, (b + 1) * blk)
        q = [q_scr[rows, sl] for sl in sls]
        k = [k_scr[rows, sl] for sl in sls]
        v = [v_scr[rows, sl] for sl in sls]
        st = [st_scr[hh] for hh in heads]
        inner = [(lax.dot_general(q[hh], k[hh], NT_DIMS, preferred_element_type=F32)
                  * wm_ref[hh]).astype(BF16) for hh in heads]
        cross = [jnp.dot(q[hh], st[hh].astype(BF16), preferred_element_type=F32) * qd_ref[hh]
                 for hh in heads]
        for hh in heads:
            kdec = (k[hh].astype(F32) * kd_ref[hh]).astype(BF16)
            st_scr[hh] = st[hh] * cd_ref[hh] + lax.dot_general(kdec, v[hh], TN_DIMS,
                                                               preferred_element_type=F32)
        for hh in heads:
            o = jnp.dot(inner[hh], v[hh], preferred_element_type=F32) + cross[hh]
            o = o * lax.rsqrt(jnp.mean(o * o, axis=-1, keepdims=True) + EPS)
            y_scr[rows, sls[hh]] = (_silu(g_scr[rows, sls[hh]].astype(F32)) * o).astype(BF16)


def _attention_tile(tile_idx, q_scr, k_scr, v_scr, bias_ref, y_scr):
    qb = q_scr.shape[0]
    lane_head = lax.broadcasted_iota(jnp.int32, (ATT_SUB, MXU_COLS), 1) // DH_ATT
    col = lax.broadcasted_iota(jnp.int32, (ATT_SUB, ATT_WIN), 1)
    for t in range(qb // ATT_SUB):
        r0 = t * ATT_SUB
        valid = col >= (1 - tile_idx) * qb - r0
        for g in range(H_ATT // HEAD_GROUP):
            cs = slice(g * MXU_COLS, (g + 1) * MXU_COLS)
            qg = q_scr[r0:r0 + ATT_SUB, cs].astype(F32)
            qs = jnp.concatenate(
                [jnp.where(lane_head == hh, qg, 0.0).astype(BF16) for hh in range(HEAD_GROUP)], axis=0)
            sc = lax.dot_general(qs, k_scr[r0:r0 + ATT_WIN, cs], NT_DIMS,
                                 preferred_element_type=F32)
            es, inv = [], []
            for hh in range(HEAD_GROUP):
                sh = sc[hh * ATT_SUB:(hh + 1) * ATT_SUB] + bias_ref[g * HEAD_GROUP + hh]
                sh = jnp.where(valid, sh, NEG_INF)
                e = jnp.exp2(sh - jnp.max(sh, axis=-1, keepdims=True))
                inv.append(1.0 / jnp.sum(e, axis=-1, keepdims=True))
                es.append(e.astype(BF16))
            pv = jnp.dot(jnp.concatenate(es, axis=0), v_scr[r0:r0 + ATT_WIN, cs],
                         preferred_element_type=F32)
            og = jnp.zeros((ATT_SUB, MXU_COLS), F32)
            for hh in range(HEAD_GROUP):
                og = jnp.where(lane_head == hh, pv[hh * ATT_SUB:(hh + 1) * ATT_SUB] * inv[hh], og)
            y_scr[r0:r0 + ATT_SUB, cs] = og.astype(BF16)


def _merge_tile(x_ref, h_scr, branches, wg_ref, wb_ref, wo_ref, m_scr, o_ref):
    d = x_ref.shape[1]
    bw = BRANCH_W
    for c in range(d // MXU_COLS):
        sl = slice(c * MXU_COLS, (c + 1) * MXU_COLS)
        acc = None
        for b, y_scr in enumerate(branches):
            gate = jax.nn.sigmoid(jnp.dot(h_scr[...], wg_ref[b * d:(b + 1) * d, sl],
                                          preferred_element_type=F32))
            term = gate * jnp.dot(y_scr[...], wb_ref[b * bw:(b + 1) * bw, sl],
                                  preferred_element_type=F32)
            acc = term if acc is None else acc + term
        m_scr[:, sl] = acc.astype(BF16)
    o_ref[...] = x_ref[...] + jnp.dot(m_scr[...], wo_ref[...], preferred_element_type=F32)


def _mixer_body(*refs, n_stage):
    n_in = 14
    (x_ref, nw_ref, win_ref, cw_ref, cos_ref, sin_ref, wm_ref, qd_ref, kd_ref, cd_ref,
     bias_ref, wg_ref, wb_ref, wo_ref) = refs[:n_in]
    stage_in = refs[n_in:n_in + n_stage]
    o_ref = refs[n_in + n_stage]
    stage_out = refs[n_in + n_stage + 1:n_in + 2 * n_stage + 1]
    (h_scr, z_scr, u_scr, yc_scr, yr_scr, ya_scr, rq_scr, rk_scr, rv_scr, rg_scr,
     aq_scr, k_scr, v_scr, st_scr, m_scr) = refs[n_in + 2 * n_stage + 1:]
    tm = x_ref.shape[0]
    j = pl.program_id(1)

    @pl.when(j == 0)
    def _():
        z_scr[0:8] = jnp.zeros((8, BRANCH_W), F32)
        st_scr[...] = jnp.zeros(st_scr.shape, F32)
        k_scr[0:tm] = jnp.zeros((tm, BRANCH_W), BF16)
        v_scr[0:tm] = jnp.zeros((tm, BRANCH_W), BF16)

    @pl.when(j > 0)
    def _():
        z_scr[0:8] = z_scr[tm:tm + 8]
        k_scr[0:tm] = k_scr[tm:2 * tm]
        v_scr[0:tm] = v_scr[tm:2 * tm]

    _cast_staged(stage_in, stage_out)
    h_scr[...] = _rms(x_ref[...], nw_ref[...]).astype(BF16)
    _project_inputs(h_scr, win_ref, cw_ref, cos_ref, sin_ref, z_scr, u_scr,
                    yc_scr, rq_scr, rk_scr, rv_scr, rg_scr, aq_scr, k_scr, v_scr)
    _retention_tile(rq_scr, rk_scr, rv_scr, rg_scr, wm_ref, qd_ref, kd_ref, cd_ref, st_scr, yr_scr)
    _attention_tile(j, aq_scr, k_scr, v_scr, bias_ref, ya_scr)
    _merge_tile(x_ref, h_scr, (yc_scr, yr_scr, ya_scr), wg_ref, wb_ref, wo_ref, m_scr, o_ref)


def _retention_tables(blk):
    f32 = np.float32
    log_gamma = np.log1p(-np.exp2(f32(-5.0) - np.arange(H_RET, dtype=f32)))
    pos = np.arange(blk, dtype=f32)
    chunk = np.arange(blk) // CHUNK
    scale = f32(DK_RET ** -0.5)
    decay = np.exp(log_gamma[:, None, None] * np.abs(pos[:, None] - pos[None, :]))
    wm = np.where(chunk[None, :, None] >= chunk[None, None, :], decay, f32(0.0)) * scale
    qd = np.exp(log_gamma[:, None] * (pos + f32(1.0))) * scale
    kd = np.exp(log_gamma[:, None] * (f32(blk - 1.0) - pos))
    cd = np.exp(log_gamma * f32(blk))
    bc = lambda a: np.ascontiguousarray(np.broadcast_to(a[..., None], a.shape + (DK_RET,)), dtype=f32)
    return tuple(jnp.asarray(t, dtype=F32) for t in (wm, bc(qd), bc(kd), bc(cd)[:, None, :]))


def _rotary_tables(s):
    f32 = np.float32
    inv_freq = f32(ROPE_BASE) ** (-np.linspace(0.0, 1.0, DK_RET // 2, dtype=f32))
    ang = np.arange(s, dtype=f32)[:, None] * inv_freq[None, :]
    cos, sin = np.cos(ang), np.sin(ang)
    cos2 = np.concatenate([cos, cos], axis=-1)
    sin2 = np.concatenate([-sin, sin], axis=-1)
    return jnp.asarray(cos2, dtype=F32), jnp.asarray(sin2, dtype=F32)


def _attention_bias(rel_bias):
    h = rel_bias.shape[0]
    back = N_PREV_CHUNKS * CHUNK
    period = ATT_WIN + ATT_SUB
    far = rel_bias[:, -1:].astype(F32)
    row = jnp.concatenate([jnp.broadcast_to(far, (h, back - REL_CLIP)),
                           rel_bias[:, ::-1].astype(F32),
                           jnp.broadcast_to(far, (h, period - back - REL_CLIP - 1))], axis=1)
    flat = jnp.broadcast_to(row[:, None, :], (h, ATT_SUB, period)).reshape(h, ATT_SUB * period)
    skew = flat[:, :ATT_SUB * (period - 1)].reshape(h, ATT_SUB, period - 1)[:, :, :ATT_WIN]
    qc = (jnp.arange(ATT_SUB) // CHUNK)[:, None]
    kc = (jnp.arange(ATT_WIN) // CHUNK)[None, :]
    band = (kc >= qc) & (kc <= qc + N_PREV_CHUNKS)
    return jnp.where(band[None], skew * LOG2E, NEG_INF)


def _mixer(x2d, bsz, nw, w_in, conv_w, cos2, sin2, ret_tables, bias, w_gate, w_branch, w_out,
           stage=()):
    t, d = x2d.shape
    s = t // bsz
    tm = MIX_TILE
    nt = s // tm
    bw = BRANCH_W
    row = pl.BlockSpec((tm, d), lambda b, j: (b * nt + j, 0))
    pos = pl.BlockSpec((tm, DK_RET), lambda b, j: (j, 0))
    consts = [w_in, conv_w, None, None, *ret_tables, bias, w_gate, w_branch, w_out]
    in_specs = [row, _const_spec((1, d))] + [pos if c is None else _const_spec(c.shape) for c in consts]
    tile = lambda dtype: pltpu.VMEM((tm, bw), dtype)
    scratch = [pltpu.VMEM((tm, d), BF16),
               pltpu.VMEM((tm + 8, bw), F32),
               tile(F32),
               tile(BF16), tile(BF16), tile(BF16),
               tile(BF16), tile(BF16), tile(BF16), tile(BF16),
               tile(BF16),
               pltpu.VMEM((2 * tm, bw), BF16), pltpu.VMEM((2 * tm, bw), BF16),
               pltpu.VMEM((H_RET, DK_RET, DK_RET), F32),
               pltpu.VMEM((tm, d), BF16)]
    st_in, st_out, st_shapes = _stage_specs(stage, bsz * nt, lambda b, j: b * nt + j)
    out = pl.pallas_call(
        functools.partial(_mixer_body, n_stage=len(stage)),
        grid=(bsz, nt),
        in_specs=in_specs + st_in,
        out_specs=[row] + st_out,
        out_shape=[jax.ShapeDtypeStruct((t, d), F32)] + st_shapes,
        scratch_shapes=scratch,
        compiler_params=_params(2, MIXER_VMEM_BYTES),
        name="mixer",
    )(x2d, nw, w_in, conv_w, cos2, sin2, *ret_tables, bias, w_gate, w_branch, w_out,
      *[w for w, _ in stage])
    return out[0], out[1:]


def kernel(x, ffn1_norm, ffn1_w_gate, ffn1_w_up, ffn1_w_down, mix_norm, w_in, conv_w, rel_bias,
           w_branch, w_merge_gate, w_out, ffn2_norm, ffn2_w_gate, ffn2_w_up, ffn2_w_down, final_norm):
    bsz, s, d = x.shape
    depth = w_in.shape[0]
    assert (bsz * s) % FFN_TILE == 0 and s % MIX_TILE == 0
    assert MIX_TILE % RET_BLOCK == 0 and RET_BLOCK % CHUNK == 0 and MIX_TILE % ATT_SUB == 0
    assert FFN_TILE % FFN_SUB == 0

    cos2, sin2 = _rotary_tables(s)
    ret_tables = _retention_tables(RET_BLOCK)

    vec = lambda w: w.reshape(1, d).astype(F32)
    gate_rows = w_merge_gate.reshape(depth, -1, d)
    branch_rows = w_branch.reshape(depth, -1, d)
    mixer_weights = lambda l: [(w_in, l), (gate_rows, l), (branch_rows, l), (w_out, l)]
    ffn1_weights = lambda l: [(ffn1_w_gate, l), (ffn1_w_up, l), (ffn1_w_down, l)]
    ffn2_weights = lambda l: [(ffn2_w_gate, l), (ffn2_w_up, l), (ffn2_w_down, l)]

    xf = x.reshape(bsz * s, d)
    ffn_w = _cast_weights(ffn1_weights(0))
    for l in range(depth):
        last = l == depth - 1
        xf, staged = _ffn(xf, vec(ffn1_norm[l]), *ffn_w, stage=mixer_weights(l) + ffn2_weights(l))
        w_in_b, w_gate_b, w_branch_b, w_out_b = staged[:4]
        ffn_w = staged[4:]
        xf, _ = _mixer(xf, bsz, vec(mix_norm[l]), w_in_b, conv_w[l].astype(F32), cos2, sin2,
                       ret_tables, _attention_bias(rel_bias[l]), w_gate_b, w_branch_b, w_out_b)
        xf, ffn_w = _ffn(xf, vec(ffn2_norm[l]), *ffn_w, fw=vec(final_norm) if last else None,
                         stage=() if last else ffn1_weights(l + 1))
    return xf.reshape(bsz, s, d)
```
